```python
import jax, jax.numpy as jnp
from jax import lax
import numpy as np

D_MODEL = 2048
BATCH = 2
SEQ = 16384
DEPTH = 4

FOURIER_GROUPS = 4
FOURIER_GROUP_DIM = 256
FOURIER_WIDTH = FOURIER_GROUPS * FOURIER_GROUP_DIM
N_HEADS = 16
N_KV_HEADS = 2
HEAD_DIM = 64
Q_WIDTH = N_HEADS * HEAD_DIM
KV_WIDTH = N_KV_HEADS * HEAD_DIM
ROPE_DIM = HEAD_DIM // 4
ROPE_THETA = 500000.0
WINDOW = 128
ATTN_BLOCK = 128
N_BRANCHES = 2
IN_WIDTH = FOURIER_WIDTH + Q_WIDTH + 2 * KV_WIDTH + N_BRANCHES * D_MODEL
N_EXPERTS = 16
N_EXPERT_GROUPS = 4
EXPERTS_PER_GROUP = N_EXPERTS // N_EXPERT_GROUPS
TOP_K = 2
D_EXPERT = 768
MOE_BLOCK = 512
N_ADA = 6
EPS = 1e-6
NEG_INF = -1e30

kernel_name = 'hybrid_fourier_swa_moe_adaln_encoder'


def rms_norm(x, g):
    x32 = x.astype(jnp.float32)
    y = x32 * lax.rsqrt(jnp.mean(x32 * x32, axis=-1, keepdims=True) + EPS)
    return (y * g.astype(jnp.float32)).astype(x.dtype)


def modulate(h, shift, scale):
    return h * (1 + scale[:, None, :]) + shift[:, None, :]


def rope_tables(positions):
    inv = ROPE_THETA ** (-jnp.arange(0, ROPE_DIM, 2, dtype=jnp.float32) / ROPE_DIM)
    ang = positions.astype(jnp.float32)[..., None] * inv
    return jnp.cos(ang), jnp.sin(ang)


def apply_partial_rope(t, cos, sin):
    half = ROPE_DIM // 2
    t32 = t.astype(jnp.float32)
    t1 = t32[..., :half]
    t2 = t32[..., half:ROPE_DIM]
    c = cos[:, :, None, :]
    s = sin[:, :, None, :]
    rot = jnp.concatenate([t1 * c - t2 * s, t2 * c + t1 * s], axis=-1).astype(t.dtype)
    return jnp.concatenate([rot, t[..., ROPE_DIM:]], axis=-1)


def fourier_mix(u):
    b, s, _ = u.shape
    ug = u.astype(jnp.float32).reshape(b, s, FOURIER_GROUPS, FOURIER_GROUP_DIM)
    f = jnp.fft.fft2(ug, axes=(1, 3), norm='ortho').real
    return f.reshape(b, s, FOURIER_WIDTH).astype(u.dtype)


def window_attention(q, k, v, sink):
    b, s_len = q.shape[:2]
    grp = N_HEADS // N_KV_HEADS
    n_blocks = s_len // ATTN_BLOCK
    qg = q.reshape(b, s_len, N_KV_HEADS, grp, HEAD_DIM)
    pad = ((0, 0), (ATTN_BLOCK, ATTN_BLOCK), (0, 0), (0, 0))
    kp = jnp.pad(k, pad)
    vp = jnp.pad(v, pad)
    sink_g = sink.astype(jnp.float32).reshape(N_KV_HEADS, grp)[None, :, :, None, None]
    scale = HEAD_DIM ** -0.5
    q_off = jnp.arange(ATTN_BLOCK)
    k_off = jnp.arange(3 * ATTN_BLOCK) - ATTN_BLOCK
    in_band = jnp.abs(k_off[None, :] - q_off[:, None]) <= WINDOW

    def block(i):
        start = i * ATTN_BLOCK
        qb = lax.dynamic_slice_in_dim(qg, start, ATTN_BLOCK, axis=1)
        kb = lax.dynamic_slice_in_dim(kp, start, 3 * ATTN_BLOCK, axis=1)
        vb = lax.dynamic_slice_in_dim(vp, start, 3 * ATTN_BLOCK, axis=1)
        kpos = start + k_off
        valid = in_band & ((kpos >= 0) & (kpos < s_len))[None, :]
        sc = jnp.einsum('bqkgd,bskd->bkgqs', qb.astype(jnp.float32), kb.astype(jnp.float32)) * scale
        sc = jnp.where(valid, sc, NEG_INF)
        m = jnp.maximum(sc.max(axis=-1, keepdims=True), sink_g)
        p = jnp.exp(sc - m)
        denom = p.sum(axis=-1, keepdims=True) + jnp.exp(sink_g - m)
        o = jnp.einsum('bkgqs,bskd->bqkgd', p / denom, vb.astype(jnp.float32))
        return o.astype(q.dtype)

    out = lax.map(block, jnp.arange(n_blocks))
    return out.transpose(1, 0, 2, 3, 4, 5).reshape(b, s_len, N_HEADS * HEAD_DIM)


def token_mixer(h, w_in, w_fo, w_ao, sink, w_out, cos, sin):
    b, s_len, _ = h.shape
    u = h @ w_in
    o1 = FOURIER_WIDTH
    o2 = o1 + Q_WIDTH
    o3 = o2 + KV_WIDTH
    o4 = o3 + KV_WIDTH
    o5 = o4 + D_MODEL
    u_f, u_q, u_k, u_v, u_gf, u_ga = jnp.split(u, [o1, o2, o3, o4, o5], axis=-1)
    y_f = fourier_mix(u_f) @ w_fo
    q = apply_partial_rope(u_q.reshape(b, s_len, N_HEADS, HEAD_DIM), cos, sin)
    k = apply_partial_rope(u_k.reshape(b, s_len, N_KV_HEADS, HEAD_DIM), cos, sin)
    v = u_v.reshape(b, s_len, N_KV_HEADS, HEAD_DIM)
    y_a = window_attention(q, k, v, sink) @ w_ao
    merged = jax.nn.sigmoid(u_gf) * y_f + jax.nn.sigmoid(u_ga) * y_a
    return merged @ w_out


def moe_ffn(h, router_w, router_b, w1, w3, w2):
    b, s_len, d = h.shape
    n_tok = b * s_len
    xt = h.reshape(n_tok, d)
    scores = jax.nn.sigmoid(xt.astype(jnp.float32) @ router_w.astype(jnp.float32))
    biased = scores + router_b.astype(jnp.float32)
    biased_g = biased.reshape(n_tok, N_EXPERT_GROUPS, EXPERTS_PER_GROUP)
    group_score = lax.top_k(biased_g, TOP_K)[0].sum(axis=-1)
    sel_group = jnp.argmax(group_score, axis=-1).astype(jnp.int32)
    in_group = jnp.take_along_axis(biased_g, sel_group[:, None, None], axis=1)[:, 0]
    _, local_idx = lax.top_k(in_group, TOP_K)
    expert_idx = sel_group[:, None] * EXPERTS_PER_GROUP + local_idx.astype(jnp.int32)
    gate_w = jnp.take_along_axis(scores, expert_idx, axis=1)
    gate_w = gate_w / gate_w.sum(axis=-1, keepdims=True)
    n_assign = n_tok * TOP_K
    e_flat = expert_idx.reshape(n_assign)
    tok_flat = jnp.repeat(jnp.arange(n_tok, dtype=jnp.int32), TOP_K)
    w_flat = gate_w.reshape(n_assign)
    order = jnp.argsort(e_flat)
    e_s = e_flat[order]
    tok_s = tok_flat[order]
    w_s = w_flat[order]
    counts = jnp.zeros((N_EXPERTS,), jnp.int32).at[e_flat].add(1)
    starts = jnp.cumsum(counts) - counts
    padded = (counts + MOE_BLOCK - 1) // MOE_BLOCK * MOE_BLOCK
    pends = jnp.cumsum(padded)
    pstarts = pends - padded
    dest = pstarts[e_s] + (jnp.arange(n_assign, dtype=jnp.int32) - starts[e_s])
    n_blocks = -(-n_assign // MOE_BLOCK) + N_EXPERTS
    cap = n_blocks * MOE_BLOCK
    slot_tok = jnp.full((cap,), n_tok, jnp.int32).at[dest].set(tok_s)
    slot_w = jnp.zeros((cap,), h.dtype).at[dest].set(w_s.astype(h.dtype))
    block_e = jnp.minimum(jnp.searchsorted(pends, jnp.arange(n_blocks) * MOE_BLOCK, side='right'), N_EXPERTS - 1)
    xpad = jnp.concatenate([xt, jnp.zeros((1, d), xt.dtype)], axis=0)
    xs = xpad[slot_tok].reshape(n_blocks, MOE_BLOCK, d)

    def expert_block(args):
        xb, e = args
        return (jax.nn.silu(xb @ w1[e]) * (xb @ w3[e])) @ w2[e]

    ys = lax.map(expert_block, (xs, block_e)).reshape(cap, d)
    out = jnp.zeros((n_tok + 1, d), h.dtype).at[slot_tok].add(ys * slot_w[:, None])[:n_tok]
    return out.reshape(b, s_len, d)


def setup_inputs(seed: int = 0) -> dict:
    key = jax.random.key(seed)
    ks = jax.random.split(key, 20)
    f32 = jnp.float32
    nrm = lambda k, shape, s: jax.random.normal(k, shape, f32) * s
    x = nrm(ks[0], (BATCH, SEQ, D_MODEL), 1.0)
    c = nrm(ks[1], (BATCH, D_MODEL), 1.0)
    positions = jnp.broadcast_to(jnp.arange(SEQ, dtype=jnp.int32), (BATCH, SEQ))
    ada_w = nrm(ks[2], (DEPTH, D_MODEL, N_ADA * D_MODEL), 0.5 * D_MODEL ** -0.5)
    ada_b = nrm(ks[3], (DEPTH, N_ADA * D_MODEL), 0.02)
    norm1_g = 1.0 + nrm(ks[4], (DEPTH, D_MODEL), 0.05)
    norm2_g = 1.0 + nrm(ks[5], (DEPTH, D_MODEL), 0.05)
    w_in = nrm(ks[6], (DEPTH, D_MODEL, IN_WIDTH), D_MODEL ** -0.5)
    w_fourier_out = nrm(ks[7], (DEPTH, FOURIER_WIDTH, D_MODEL), FOURIER_WIDTH ** -0.5)
    w_attn_out = nrm(ks[8], (DEPTH, Q_WIDTH, D_MODEL), Q_WIDTH ** -0.5)
    attn_sink = nrm(ks[9], (DEPTH, N_HEADS), 1.0)
    w_out = nrm(ks[10], (DEPTH, D_MODEL, D_MODEL), D_MODEL ** -0.5)
    router_w = nrm(ks[11], (D_MODEL, N_EXPERTS), D_MODEL ** -0.5)
    router_b = nrm(ks[12], (N_EXPERTS,), 0.01)
    expert_w1 = nrm(ks[13], (DEPTH, N_EXPERTS, D_MODEL, D_EXPERT), D_MODEL ** -0.5)
    expert_w3 = nrm(ks[14], (DEPTH, N_EXPERTS, D_MODEL, D_EXPERT), D_MODEL ** -0.5)
    expert_w2 = nrm(ks[15], (DEPTH, N_EXPERTS, D_EXPERT, D_MODEL), D_EXPERT ** -0.5)
    final_norm_g = 1.0 + nrm(ks[16], (D_MODEL,), 0.05)
    return {'x': x, 'c': c, 'positions': positions, 'ada_w': ada_w, 'ada_b': ada_b,
            'norm1_g': norm1_g, 'norm2_g': norm2_g, 'w_in': w_in, 'w_fourier_out': w_fourier_out,
            'w_attn_out': w_attn_out, 'attn_sink': attn_sink, 'w_out': w_out,
            'router_w': router_w, 'router_b': router_b, 'expert_w1': expert_w1,
            'expert_w3': expert_w3, 'expert_w2': expert_w2, 'final_norm_g': final_norm_g}


def reference(x, c, positions, ada_w, ada_b, norm1_g, norm2_g, w_in, w_fourier_out,
              w_attn_out, attn_sink, w_out, router_w, router_b, expert_w1, expert_w3,
              expert_w2, final_norm_g):
    cos, sin = rope_tables(positions)
    c_act = jax.nn.silu(c)
    for l in range(DEPTH):
        mod = c_act @ ada_w[l] + ada_b[l]
        sh1, sc1, g1, sh2, sc2, g2 = jnp.split(mod, N_ADA, axis=-1)
        h = modulate(rms_norm(x, norm1_g[l]), sh1, sc1)
        x = x + g1[:, None, :] * token_mixer(h, w_in[l], w_fourier_out[l], w_attn_out[l],
                                             attn_sink[l], w_out[l], cos, sin)
        h = modulate(rms_norm(x, norm2_g[l]), sh2, sc2)
        x = x + g2[:, None, :] * moe_ffn(h, router_w, router_b, expert_w1[l], expert_w3[l], expert_w2[l])
    return rms_norm(x, final_norm_g)
```

```python
import functools
import math

import numpy as np
import jax
import jax.numpy as jnp
from jax import lax
from jax.experimental import pallas as pl
from jax.experimental.pallas import tpu as pltpu

F32 = jnp.float32
BF16 = jnp.bfloat16
I32 = jnp.int32

D_MODEL = 2048
N_ADA = 6
F_GROUPS = 4
F_GDIM = 256
F_WIDTH = F_GROUPS * F_GDIM
N_HEADS = 16
N_KV = 2
HEAD_DIM = 64
HEADS_PER_KV = N_HEADS // N_KV
Q_WIDTH = N_HEADS * HEAD_DIM
KV_WIDTH = N_KV * HEAD_DIM
ROPE_DIM = HEAD_DIM // 4
ROPE_HALF = ROPE_DIM // 2
ROPE_THETA = 500000.0
WINDOW = 128
N_EXPERTS = 16
N_GROUPS = 4
EPG = N_EXPERTS // N_GROUPS
TOP_K = 2
D_EXPERT = 768
MOE_BLOCK = 512
EPS = 1e-6
NEG_INF = -1e30
IN_WIDTH = F_WIDTH + Q_WIDTH + 2 * KV_WIDTH + 2 * D_MODEL

COL_GF = (F_WIDTH + Q_WIDTH) // D_MODEL
COL_GA = COL_GF + 1
COL_K = (F_WIDTH + Q_WIDTH + 2 * D_MODEL) // KV_WIDTH
COL_V = COL_K + 1

LANES = 128
SUBLANES = 8
BF16_ROWS = 16
VMEM_LIMIT = 56 * 1024 * 1024

ATT_Q = 256
ATT_KW = ATT_Q + 2 * WINDOW


def _cparams(sem, vmem=VMEM_LIMIT):
    return pltpu.CompilerParams(dimension_semantics=sem, vmem_limit_bytes=vmem)


def _tile(n, pref):
    t = min(n, pref)
    while n % t:
        t //= 2
    return t


ADA_TN = 768
ADA_ROWS = 32


def _ada_kernel(cb_ref, w_ref, b_ref, o_ref, *, n_batch):
    tn = w_ref.shape[-1]
    n_chunks = w_ref.shape[1] // ADA_ROWS

    def body(i, accs):
        r = pl.multiple_of(i * ADA_ROWS, ADA_ROWS)
        w = w_ref[0, pl.ds(r, ADA_ROWS), :].reshape(ADA_ROWS // SUBLANES, SUBLANES, tn)
        out = []
        for b in range(n_batch):
            c = cb_ref[b, pl.ds(r, ADA_ROWS), :]
            ca = c * jax.nn.sigmoid(c)
            cr = jnp.tile(ca, (1, tn // LANES)).reshape(ADA_ROWS // SUBLANES, SUBLANES, tn)
            out.append(accs[b] + jnp.sum(w * cr, axis=0))
        return tuple(out)

    accs = lax.fori_loop(0, n_chunks, body,
                         tuple(jnp.zeros((SUBLANES, tn), F32) for _ in range(n_batch)))
    for b in range(n_batch):
        o_ref[0, b:b + 1, :] = jnp.sum(accs[b], axis=0, keepdims=True) + b_ref[0]


def _ada_mod(c, ada_w, ada_b):
    n_layers, d, width = ada_w.shape
    n_batch = c.shape[0]
    cb = jnp.broadcast_to(c[:, :, None], (n_batch, d, LANES))
    return pl.pallas_call(
        functools.partial(_ada_kernel, n_batch=n_batch),
        grid=(n_layers, width // ADA_TN),
        in_specs=[
            pl.BlockSpec((n_batch, d, LANES), lambda l, j: (0, 0, 0)),
            pl.BlockSpec((1, d, ADA_TN), lambda l, j: (l, 0, j)),
            pl.BlockSpec((1, 1, ADA_TN), lambda l, j: (l, 0, j)),
        ],
        out_specs=pl.BlockSpec((1, n_batch, ADA_TN), lambda l, j: (l, 0, j)),
        out_shape=jax.ShapeDtypeStruct((n_layers, n_batch, width), F32),
        compiler_params=_cparams(("arbitrary", "arbitrary")),
        name="ada_mod",
    )(cb, ada_w, ada_b.reshape(n_layers, 1, width))


NORM_ROWS = 16


def _norm_mod_rows(x, g, shift, scale):
    ms = jnp.mean(x * x, axis=-1, keepdims=True)
    y = x * lax.rsqrt(ms + EPS) * g
    return y * (1.0 + scale) + shift


A_TM = 1024
A_TN = 1280


def _a_kernel(x_ref, mod_ref, g_ref, w_ref, o_ref, h_ref):
    @pl.when(pl.program_id(1) == 0)
    def _():
        g = g_ref[...]
        shift = mod_ref[0, 0:1, :]
        scale = mod_ref[0, 1:2, :]

        def body(i, carry):
            r = pl.multiple_of(i * NORM_ROWS, NORM_ROWS)
            h = _norm_mod_rows(x_ref[pl.ds(r, NORM_ROWS), :], g, shift, scale)
            h_ref[pl.ds(r, NORM_ROWS), :] = h.astype(BF16)
            return carry

        lax.fori_loop(0, x_ref.shape[0] // NORM_ROWS, body, 0)

    o_ref[...] = jnp.dot(h_ref[...], w_ref[...], preferred_element_type=F32).astype(o_ref.dtype)


def _in_proj(x2, mod3, g, w_bf, seq):
    n, d = x2.shape
    width = w_bf.shape[1]
    tm = _tile(seq, A_TM)
    tn = _tile(width, A_TN)
    per_batch = seq // tm
    return pl.pallas_call(
        _a_kernel,
        grid=(n // tm, width // tn),
        in_specs=[
            pl.BlockSpec((tm, d), lambda i, j: (i, 0)),
            pl.BlockSpec((1, N_ADA, d), lambda i, j: (i // per_batch, 0, 0)),
            pl.BlockSpec((1, d), lambda i, j: (0, 0)),
            pl.BlockSpec((d, tn), lambda i, j: (0, j)),
        ],
        out_specs=pl.BlockSpec((tm, tn), lambda i, j: (i, j)),
        out_shape=jax.ShapeDtypeStruct((n, width), BF16),
        scratch_shapes=[pltpu.VMEM((tm, d), BF16)],
        compiler_params=_cparams(("arbitrary", "arbitrary")),
        name="in_proj",
    )(x2, mod3, g, w_bf)


ROPE_TM = 512


def _rope_kernel(q_ref, k_ref, c_ref, sa_ref, sb_ref, qo_ref, ko_ref):
    c = c_ref[...]
    sa = sa_ref[...]
    sb = sb_ref[...]

    def rope(t):
        return (t * c + pltpu.roll(t, LANES - ROPE_HALF, 1) * sa
                + pltpu.roll(t, ROPE_HALF, 1) * sb)

    ko_ref[...] = rope(k_ref[...].astype(F32)).astype(BF16)
    for m in range(Q_WIDTH // LANES):
        q = q_ref[:, m * LANES:(m + 1) * LANES].astype(F32)
        qo_ref[:, m * LANES:(m + 1) * LANES] = (rope(q) * (HEAD_DIM ** -0.5)).astype(BF16)


def _rope(u, tabs):
    n = u.shape[0]
    tm = _tile(n, ROPE_TM)
    tab_spec = pl.BlockSpec((tm, LANES), lambda i: (i, 0))
    return pl.pallas_call(
        _rope_kernel,
        grid=(n // tm,),
        in_specs=[
            pl.BlockSpec((tm, Q_WIDTH), lambda i: (i, F_WIDTH // Q_WIDTH)),
            pl.BlockSpec((tm, KV_WIDTH), lambda i: (i, COL_K)),
            tab_spec, tab_spec, tab_spec,
        ],
        out_specs=[
            pl.BlockSpec((tm, Q_WIDTH), lambda i: (i, 0)),
            pl.BlockSpec((tm, KV_WIDTH), lambda i: (i, 0)),
        ],
        out_shape=[jax.ShapeDtypeStruct((n, Q_WIDTH), BF16),
                   jax.ShapeDtypeStruct((n, KV_WIDTH), BF16)],
        compiler_params=_cparams(("arbitrary",)),
        name="rope",
    )(u, u, *tabs)


def _rope_tables(positions):
    inv = ROPE_THETA ** (-jnp.arange(0, ROPE_DIM, 2, dtype=F32) / ROPE_DIM)
    ang = positions.astype(F32)[..., None] * inv
    cos = jnp.cos(ang).reshape(-1, ROPE_HALF)
    sin = jnp.sin(ang).reshape(-1, ROPE_HALF)
    n = cos.shape[0]
    rest = HEAD_DIM - ROPE_DIM
    ones = jnp.ones((n, rest), F32)
    z_half = jnp.zeros((n, ROPE_HALF), F32)
    z_rest = jnp.zeros((n, rest), F32)
    reps = LANES // HEAD_DIM
    c = jnp.tile(jnp.concatenate([cos, cos, ones], axis=1), (1, reps))
    sa = jnp.tile(jnp.concatenate([-sin, z_half, z_rest], axis=1), (1, reps))
    sb = jnp.tile(jnp.concatenate([z_half, sin, z_rest], axis=1), (1, reps))
    return c, sa, sb


F_SUB = BF16_ROWS


def _dft_constants(n1_len, n2_len):
    cidx = np.arange(F_GDIM)
    ang = 2.0 * np.pi * np.outer(cidx, cidx) / F_GDIM
    wc = np.concatenate([np.cos(ang), -np.sin(ang)], axis=1) / math.sqrt(F_GDIM)
    a = np.arange(n1_len)
    ang1 = 2.0 * np.pi * np.outer(a, a) / n1_len
    fr, fi = np.cos(ang1), -np.sin(ang1)
    fbig = np.block([[fr, -fi], [fi, fr]]) / math.sqrt(n1_len)
    k1 = np.arange(n1_len)[:, None, None]
    k2 = np.arange(n2_len)[None, :, None]
    b = np.arange(n2_len)[None, None, :]
    ang2 = 2.0 * np.pi * (b * k2 / n2_len + b * k1 / (n1_len * n2_len))
    g = np.concatenate([np.cos(ang2), np.sin(ang2)], axis=2) / math.sqrt(n2_len)
    return (jnp.asarray(wc, BF16), jnp.asarray(fbig, BF16), jnp.asarray(g, BF16))


def _f12_kernel(u_ref, wc_ref, fb_ref, t_ref):
    n1_len = u_ref.shape[1]
    x = u_ref[0].reshape(n1_len * F_SUB, F_WIDTH)
    zr, zi = [], []
    for g in range(F_GROUPS):
        r = jnp.dot(x[:, g * F_GDIM:(g + 1) * F_GDIM], wc_ref[...], preferred_element_type=F32)
        zr.append(r[:, :F_GDIM].astype(BF16))
        zi.append(r[:, F_GDIM:].astype(BF16))
    zr = jnp.concatenate(zr, axis=1).reshape(n1_len, F_SUB, F_WIDTH)
    zi = jnp.concatenate(zi, axis=1).reshape(n1_len, F_SUB, F_WIDTH)
    zr = pltpu.einshape("mnc->nmc", zr)
    zi = pltpu.einshape("mnc->nmc", zi)
    for j in range(F_SUB):
        rhs = jnp.concatenate([zr[j], zi[j]], axis=0)
        t = jnp.dot(fb_ref[...], rhs, preferred_element_type=F32)
        t_ref[0, j] = t.astype(BF16).reshape(2, n1_len, F_WIDTH)


def _f3_kernel(t_ref, g_ref, y_ref):
    tr = pltpu.einshape("nkc->knc", t_ref[0, :, 0])
    ti = pltpu.einshape("nkc->knc", t_ref[0, :, 1])
    ys = []
    for j in range(F_SUB):
        rhs = jnp.concatenate([tr[j], ti[j]], axis=0)
        ys.append(jnp.dot(g_ref[j], rhs, preferred_element_type=F32).astype(BF16))
    y_ref[0] = pltpu.einshape("knc->nkc", jnp.stack(ys, axis=0))


def _fourier(u, n_batch, seq, consts):
    wc, fbig, gmat = consts
    n1_len = fbig.shape[0] // 2
    n2_len = seq // n1_len
    u5 = u.reshape(n_batch, n1_len, n2_len, u.shape[-1])
    t = pl.pallas_call(
        _f12_kernel,
        grid=(n_batch, n2_len // F_SUB),
        in_specs=[
            pl.BlockSpec((1, n1_len, F_SUB, F_WIDTH), lambda b, j: (b, 0, j, 0)),
            pl.BlockSpec(wc.shape, lambda b, j: (0, 0)),
            pl.BlockSpec(fbig.shape, lambda b, j: (0, 0)),
        ],
        out_specs=pl.BlockSpec((1, F_SUB, 2, n1_len, F_WIDTH), lambda b, j: (b, j, 0, 0, 0)),
        out_shape=jax.ShapeDtypeStruct((n_batch, n2_len, 2, n1_len, F_WIDTH), BF16),
        compiler_params=_cparams(("arbitrary", "arbitrary")),
        name="fourier_stage1",
    )(u5, wc, fbig)
    y = pl.pallas_call(
        _f3_kernel,
        grid=(n_batch, n1_len // F_SUB),
        in_specs=[
            pl.BlockSpec((1, n2_len, 2, F_SUB, F_WIDTH), lambda b, j: (b, 0, 0, j, 0)),
            pl.BlockSpec((F_SUB, n2_len, 2 * n2_len), lambda b, j: (j, 0, 0)),
        ],
        out_specs=pl.BlockSpec((1, n2_len, F_SUB, F_WIDTH), lambda b, j: (b, 0, j, 0)),
        out_shape=jax.ShapeDtypeStruct((n_batch, n2_len, n1_len, F_WIDTH), BF16),
        compiler_params=_cparams(("arbitrary", "arbitrary")),
        name="fourier_stage2",
    )(t, gmat)
    return y.reshape(n_batch * seq, F_WIDTH)


def _att_kernel(sink_ref, q_ref, k_ref, v_ref, o_ref, *, seq):
    i = pl.program_id(1)
    s0 = i * ATT_Q
    ws = pl.multiple_of(jnp.clip(s0 - WINDOW, 0, seq - ATT_KW), WINDOW)
    kw = k_ref[0, pl.ds(ws, ATT_KW), :].astype(F32)
    vw = v_ref[0, pl.ds(ws, ATT_KW), :].astype(F32)
    lane = lax.broadcasted_iota(I32, (ATT_KW, LANES), 1)
    low = lane < HEAD_DIM
    kw_sw = pltpu.roll(kw, HEAD_DIM, 1)
    vw_sw = pltpu.roll(vw, HEAD_DIM, 1)
    row = lax.broadcasted_iota(I32, (ATT_Q, ATT_KW), 0)
    col = lax.broadcasted_iota(I32, (ATT_Q, ATT_KW), 1)
    rel = (ws + col) - (s0 + row)
    bias = jnp.where(jnp.abs(rel) <= WINDOW, 0.0, NEG_INF).astype(F32)
    for h in range(N_KV):
        k_own, k_oth = (kw, kw_sw) if h == 0 else (kw_sw, kw)
        v_own, v_oth = (vw, vw_sw) if h == 0 else (vw_sw, vw)
        k_lo = jnp.where(low, k_own, 0.0).astype(BF16)
        k_hi = jnp.where(low, 0.0, k_oth).astype(BF16)
        v_lo = jnp.where(low, v_own, 0.0).astype(BF16)
        v_hi = jnp.where(low, 0.0, v_oth).astype(BF16)
        for m in range(HEADS_PER_KV // 2):
            cb = (h * HEADS_PER_KV // 2 + m) * LANES
            q2 = q_ref[:, cb:cb + LANES]
            acc = jnp.zeros((ATT_Q, LANES), F32)
            for half, (kk, vv) in enumerate(((k_lo, v_lo), (k_hi, v_hi))):
                sink = sink_ref[h * HEADS_PER_KV + 2 * m + half]
                s = lax.dot_general(q2, kk, (((1,), (1,)), ((), ())),
                                    preferred_element_type=F32) + bias
                mx = jnp.maximum(jnp.max(s, axis=-1, keepdims=True), sink)
                p = jnp.exp(s - mx)
                den = jnp.sum(p, axis=-1, keepdims=True) + jnp.exp(sink - mx)
                o = jnp.dot(p.astype(BF16), vv, preferred_element_type=F32)
                acc = acc + o / den
            o_ref[:, cb:cb + LANES] = acc.astype(BF16)


def _attention(q_rot, k_rot, u, sink, n_batch, seq):
    n = q_rot.shape[0]
    nq = seq // ATT_Q
    k3 = k_rot.reshape(n_batch, seq, KV_WIDTH)
    u3 = u.reshape(n_batch, seq, u.shape[-1])
    return pl.pallas_call(
        functools.partial(_att_kernel, seq=seq),
        grid=(n_batch, nq),
        in_specs=[
            pl.BlockSpec(memory_space=pltpu.SMEM),
            pl.BlockSpec((ATT_Q, Q_WIDTH), lambda b, i: (b * nq + i, 0)),
            pl.BlockSpec((1, seq, KV_WIDTH), lambda b, i: (b, 0, 0)),
            pl.BlockSpec((1, seq, KV_WIDTH), lambda b, i: (b, 0, COL_V)),
        ],
        out_specs=pl.BlockSpec((ATT_Q, Q_WIDTH), lambda b, i: (b * nq + i, 0)),
        out_shape=jax.ShapeDtypeStruct((n, Q_WIDTH), BF16),
        compiler_params=_cparams(("arbitrary", "arbitrary")),
        name="window_attention",
    )(sink, q_rot, k3, u3)


D_TM = 256
ROUTER_PAD = LANES


def _d_kernel(f_ref, a_ref, gf_ref, ga_ref, x_ref, mod_ref, g2_ref,
              wfo_ref, wao_ref, wout_ref, wr_ref, x1_ref, h2_ref, lg_ref):
    yf = jnp.dot(f_ref[...], wfo_ref[...], preferred_element_type=F32)
    ya = jnp.dot(a_ref[...], wao_ref[...], preferred_element_type=F32)
    merged = (jax.nn.sigmoid(gf_ref[...].astype(F32)) * yf
              + jax.nn.sigmoid(ga_ref[...].astype(F32)) * ya)
    out = jnp.dot(merged.astype(BF16), wout_ref[...], preferred_element_type=F32)
    x1_ref[...] = x_ref[...] + mod_ref[0, 2:3, :] * out
    g2 = g2_ref[...]
    shift = mod_ref[0, 3:4, :]
    scale = mod_ref[0, 4:5, :]

    def body(i, carry):
        r = pl.multiple_of(i * NORM_ROWS, NORM_ROWS)
        h2_ref[pl.ds(r, NORM_ROWS), :] = _norm_mod_rows(
            x1_ref[pl.ds(r, NORM_ROWS), :], g2, shift, scale)
        return carry

    lax.fori_loop(0, x1_ref.shape[0] // NORM_ROWS, body, 0)
    lg_ref[...] = jnp.dot(h2_ref[...].astype(BF16), wr_ref[...], preferred_element_type=F32)


def _const_spec(shape):
    nd = len(shape)
    return pl.BlockSpec(shape, lambda i: (0,) * nd, pipeline_mode=pl.Buffered(1))


def _mix_out(yf, att, u, x2, mod3, g2, wfo, wao, wout, wr, seq):
    n, d = x2.shape
    tm = _tile(seq, D_TM)
    per_batch = seq // tm
    row = lambda w: pl.BlockSpec((tm, w), lambda i: (i, 0))
    return pl.pallas_call(
        _d_kernel,
        grid=(n // tm,),
        in_specs=[
            row(F_WIDTH), row(Q_WIDTH),
            pl.BlockSpec((tm, d), lambda i: (i, COL_GF)),
            pl.BlockSpec((tm, d), lambda i: (i, COL_GA)),
            row(d),
            pl.BlockSpec((1, N_ADA, d), lambda i: (i // per_batch, 0, 0)),
            pl.BlockSpec((1, d), lambda i: (0, 0)),
            _const_spec(wfo.shape), _const_spec(wao.shape), _const_spec(wout.shape),
            _const_spec(wr.shape),
        ],
        out_specs=[row(d), row(d), row(ROUTER_PAD)],
        out_shape=[jax.ShapeDtypeStruct((n, d), F32), jax.ShapeDtypeStruct((n, d), F32),
                   jax.ShapeDtypeStruct((n, ROUTER_PAD), F32)],
        compiler_params=_cparams(("arbitrary",)),
        name="mix_out",
    )(yf, att, u, u, x2, mod3, g2, wfo, wao, wout, wr)


R_TN = 2048
R_SUB = 512


def _route_kernel(bias_ref, lg_ref, tri_ref, e_ref, w_ref, p_ref, cnt_ref, carry_ref):
    @pl.when(pl.program_id(0) == 0)
    def _():
        carry_ref[...] = jnp.zeros_like(carry_ref)

    tn = lg_ref.shape[0]
    lt = lg_ref[...].T
    sc = [jax.nn.sigmoid(lt[e:e + 1, :]) for e in range(N_EXPERTS)]
    bz = [sc[e] + bias_ref[e] for e in range(N_EXPERTS)]
    one = jnp.ones((1, tn), F32)
    zero = jnp.zeros((1, tn), F32)
    in_top, gscore = [], []
    for g in range(N_GROUPS):
        v = bz[g * EPG:(g + 1) * EPG]
        rank = [zero] * EPG
        for a in range(EPG):
            for b in range(a + 1, EPG):
                a_wins = jnp.where(v[a] >= v[b], one, zero)
                rank[b] = rank[b] + a_wins
                rank[a] = rank[a] + (one - a_wins)
        top = [jnp.where(rank[a] < TOP_K - 0.5, one, zero) for a in range(EPG)]
        in_top += top
        gs = zero
        for a in range(EPG):
            gs = gs + top[a] * v[a]
        gscore.append(gs)
    sel = []
    for g in range(N_GROUPS):
        ok = one
        for g2 in range(N_GROUPS):
            if g2 < g:
                ok = ok * jnp.where(gscore[g] > gscore[g2], one, zero)
            elif g2 > g:
                ok = ok * jnp.where(gscore[g] >= gscore[g2], one, zero)
        sel += [in_top[g * EPG + a] * ok for a in range(EPG)]

    selm = jnp.concatenate(sel, axis=0)
    carry = carry_ref[:, 0:1]
    cums = []
    for c in range(tn // R_SUB):
        blk = selm[:, c * R_SUB:(c + 1) * R_SUB]
        cums.append(jnp.dot(blk.astype(BF16), tri_ref[...], preferred_element_type=F32) + carry)
        carry = carry + jnp.sum(blk, axis=1, keepdims=True)
    cum = jnp.concatenate(cums, axis=1)
    carry_ref[...] = jnp.broadcast_to(carry, carry_ref.shape)
    cnt_ref[...] = jnp.broadcast_to(carry, cnt_ref.shape)

    before = zero
    eidx = [zero, zero]
    score = [zero, zero]
    pos = [zero, zero]
    for e in range(N_EXPERTS):
        pick = [sel[e] * jnp.where(before < 0.5, one, zero),
                sel[e] * jnp.where(before >= 0.5, one, zero)]
        for k in range(TOP_K):
            eidx[k] = eidx[k] + pick[k] * float(e)
            score[k] = score[k] + pick[k] * sc[e]
            pos[k] = pos[k] + pick[k] * cum[e:e + 1, :]
        before = before + sel[e]
    tot = score[0] + score[1]
    for k in range(TOP_K):
        e_ref[k:k + 1, :] = eidx[k].astype(I32)
        w_ref[k:k + 1, :] = score[k] / tot
        p_ref[k:k + 1, :] = pos[k].astype(I32)


def _route(logits, router_b):
    n = logits.shape[0]
    tn = _tile(n, R_TN)
    tri = jnp.asarray(np.triu(np.ones((R_SUB, R_SUB), np.float32), k=1), BF16)
    kn = lambda: pl.BlockSpec((TOP_K, tn), lambda i: (0, i))
    return pl.pallas_call(
        _route_kernel,
        grid=(n // tn,),
        in_specs=[
            pl.BlockSpec(memory_space=pltpu.SMEM),
            pl.BlockSpec((tn, ROUTER_PAD), lambda i: (i, 0)),
            pl.BlockSpec((R_SUB, R_SUB), lambda i: (0, 0)),
        ],
        out_specs=[kn(), kn(), kn(), pl.BlockSpec((N_EXPERTS, LANES), lambda i: (0, 0))],
        out_shape=[jax.ShapeDtypeStruct((TOP_K, n), I32), jax.ShapeDtypeStruct((TOP_K, n), F32),
                   jax.ShapeDtypeStruct((TOP_K, n), I32),
                   jax.ShapeDtypeStruct((N_EXPERTS, LANES), F32)],
        scratch_shapes=[pltpu.VMEM((N_EXPERTS, LANES), F32)],
        compiler_params=_cparams(("arbitrary",)),
        name="route",
    )(router_b, logits, tri)


P_TM = 256
P_ZROWS = 256


def _row_copy(src, src_row, dst, dst_row, sem):
    return pltpu.make_async_copy(src.at[pl.ds(src_row, 1), :], dst.at[pl.ds(dst_row, 1), :], sem)


def _dispatch_kernel(padlo_ref, padhi_ref, dest_ref, h_ref, xs_ref, zrow_ref, sem, zsem):
    tm = h_ref.shape[0]

    def issue(r, carry):
        for k in range(TOP_K):
            _row_copy(h_ref, r, xs_ref, dest_ref[k, r], sem).start()
        return carry

    lax.fori_loop(0, tm, issue, 0)

    @pl.when(pl.program_id(0) == pl.num_programs(0) - 1)
    def _():
        zrow_ref[...] = jnp.zeros_like(zrow_ref)
        zrows = zrow_ref.shape[0]

        def zissue(r, carry):
            _row_copy(zrow_ref, 0, xs_ref, r, zsem).start()
            return carry

        def zwait(r, carry):
            _row_copy(zrow_ref, 0, xs_ref, r, zsem).wait()
            return carry

        for e in range(N_EXPERTS):
            lax.fori_loop(padlo_ref[e], padhi_ref[e], zissue, 0)
            lax.fori_loop(padlo_ref[e], padhi_ref[e], zwait, 0)

        def chunk(j):
            start = pl.multiple_of(padhi_ref[N_EXPERTS - 1] + j * zrows, zrows)
            return pltpu.make_async_copy(zrow_ref, xs_ref.at[pl.ds(start, zrows), :], zsem)

        n_chunks = (xs_ref.shape[0] - padhi_ref[N_EXPERTS - 1]) // zrows
        lax.fori_loop(0, n_chunks, lambda j, c: (chunk(j).start(), c)[1], 0)
        lax.fori_loop(0, n_chunks, lambda j, c: (chunk(j).wait(), c)[1], 0)

    for k in range(TOP_K):
        pltpu.make_async_copy(h_ref, xs_ref.at[pl.ds(0, tm), :], sem).wait()


def _dispatch(h2, dest, pad_lo, pad_hi, cap):
    n, d = h2.shape
    tm = _tile(n, P_TM)
    grid_spec = pltpu.PrefetchScalarGridSpec(
        num_scalar_prefetch=2,
        grid=(n // tm,),
        in_specs=[pl.BlockSpec((TOP_K, tm), lambda i, *_: (0, i), memory_space=pltpu.SMEM),
                  pl.BlockSpec((tm, d), lambda i, *_: (i, 0))],
        out_specs=pl.BlockSpec(memory_space=pl.ANY),
        scratch_shapes=[pltpu.VMEM((P_ZROWS, d), F32), pltpu.SemaphoreType.DMA(()),
                        pltpu.SemaphoreType.DMA(())],
    )
    return pl.pallas_call(
        _dispatch_kernel,
        grid_spec=grid_spec,
        out_shape=jax.ShapeDtypeStruct((cap, d), F32),
        compiler_params=_cparams(("arbitrary",)),
        name="dispatch",
    )(pad_lo, pad_hi, dest, h2)


def _expert_kernel(be_ref, nu_ref, xs_ref, w1_ref, w3_ref, w2_ref, ys_ref):
    @pl.when(pl.program_id(0) < nu_ref[0])
    def _():
        x = xs_ref[...].astype(BF16)
        a = jnp.dot(x, w1_ref[0], preferred_element_type=F32)
        b = jnp.dot(x, w3_ref[0], preferred_element_type=F32)
        h = (a * jax.nn.sigmoid(a)) * b
        ys_ref[...] = jnp.dot(h.astype(BF16), w2_ref[0], preferred_element_type=F32)

    @pl.when(pl.program_id(0) >= nu_ref[0])
    def _():
        ys_ref[...] = jnp.zeros_like(ys_ref)


def _experts(xs, block_e, n_used, w1, w3, w2):
    cap, d = xs.shape
    n_blocks = cap // MOE_BLOCK
    de = w1.shape[-1]
    blk = lambda i, be, nu: (jnp.minimum(i, nu[0] - 1), 0)
    wsel = lambda i, be, nu: (be[jnp.minimum(i, nu[0] - 1)], 0, 0)
    grid_spec = pltpu.PrefetchScalarGridSpec(
        num_scalar_prefetch=2,
        grid=(n_blocks,),
        in_specs=[
            pl.BlockSpec((MOE_BLOCK, d), blk),
            pl.BlockSpec((1, d, de), wsel),
            pl.BlockSpec((1, d, de), wsel),
            pl.BlockSpec((1, de, d), wsel),
        ],
        out_specs=pl.BlockSpec((MOE_BLOCK, d), lambda i, be, nu: (i, 0)),
    )
    return pl.pallas_call(
        _expert_kernel,
        grid_spec=grid_spec,
        out_shape=jax.ShapeDtypeStruct((cap, d), F32),
        compiler_params=_cparams(("arbitrary",)),
        name="experts",
    )(block_e, n_used, xs, w1, w3, w2)


C_TM = 256


def _combine_kernel(dest_ref, x_ref, wb_ref, mod_ref, fg_ref, ys_ref, o_ref, buf_ref, sem,
                    *, final):
    tm = x_ref.shape[0]

    def issue(r, carry):
        for k in range(TOP_K):
            _row_copy(ys_ref, dest_ref[k, r], buf_ref.at[k], r, sem).start()
        return carry

    lax.fori_loop(0, tm, issue, 0)
    for k in range(TOP_K):
        pltpu.make_async_copy(ys_ref.at[pl.ds(0, tm), :], buf_ref.at[k], sem).wait()

    reps = x_ref.shape[1] // LANES
    y = (jnp.tile(wb_ref[0], (1, reps)) * buf_ref[0]
         + jnp.tile(wb_ref[1], (1, reps)) * buf_ref[1])
    out = x_ref[...] + mod_ref[0, 5:6, :] * y
    if final:
        ms = jnp.mean(out * out, axis=-1, keepdims=True)
        out = out * lax.rsqrt(ms + EPS) * fg_ref[...]
    o_ref[...] = out


def _combine(x1, ys, dest, wb, mod3, fg, seq, final):
    n, d = x1.shape
    tm = _tile(seq, C_TM)
    per_batch = seq // tm
    grid_spec = pltpu.PrefetchScalarGridSpec(
        num_scalar_prefetch=0,
        grid=(n // tm,),
        in_specs=[
            pl.BlockSpec((TOP_K, tm), lambda i, *_: (0, i), memory_space=pltpu.SMEM),
            pl.BlockSpec((tm, d), lambda i, *_: (i, 0)),
            pl.BlockSpec((TOP_K, tm, LANES), lambda i, *_: (0, i, 0)),
            pl.BlockSpec((1, N_ADA, d), lambda i, *_: (i // per_batch, 0, 0)),
            pl.BlockSpec((1, d), lambda i, *_: (0, 0)),
            pl.BlockSpec(memory_space=pl.ANY),
        ],
        out_specs=pl.BlockSpec((tm, d), lambda i, *_: (i, 0)),
        scratch_shapes=[pltpu.VMEM((TOP_K, tm, d), F32), pltpu.SemaphoreType.DMA(())],
    )
    return pl.pallas_call(
        functools.partial(_combine_kernel, final=final),
        grid_spec=grid_spec,
        out_shape=jax.ShapeDtypeStruct((n, d), F32),
        compiler_params=_cparams(("arbitrary",)),
        name="combine",
    )(dest, x1, wb, mod3, fg, ys)


def _moe(h2, logits, x1, mod3, fg, router_b, w1, w3, w2, seq, final):
    n, d = h2.shape
    eidx, gw, pos, cnt = _route(logits, router_b)
    counts = cnt[:, 0].astype(I32)
    padded = (counts + MOE_BLOCK - 1) // MOE_BLOCK * MOE_BLOCK
    pends = jnp.cumsum(padded)
    pstarts = pends - padded
    n_blocks = -(-(n * TOP_K) // MOE_BLOCK) + N_EXPERTS
    block_e = jnp.minimum(
        jnp.searchsorted(pends, jnp.arange(n_blocks, dtype=I32) * MOE_BLOCK, side="right"),
        N_EXPERTS - 1).astype(I32)
    n_used = (pends[-1:] // MOE_BLOCK).astype(I32)
    dest = (jnp.take(pstarts, eidx) + pos).astype(I32)
    xs = _dispatch(h2, dest, (pstarts + counts).astype(I32), pends.astype(I32),
                   n_blocks * MOE_BLOCK)
    ys = _experts(xs, block_e, n_used, w1, w3, w2)
    wb = jnp.broadcast_to(gw[:, :, None], (TOP_K, n, LANES))
    return _combine(x1, ys, dest, wb, mod3, fg, seq, final)


def _fft_split(seq):
    n1_len = 1 << (int(math.log2(seq)) // 2)
    return n1_len, seq // n1_len


def kernel(x, c, positions, ada_w, ada_b, norm1_g, norm2_g, w_in, w_fourier_out, w_attn_out,
           attn_sink, w_out, router_w, router_b, expert_w1, expert_w3, expert_w2, final_norm_g):
    n_batch, seq, d = x.shape
    depth = ada_w.shape[0]
    n1_len, n2_len = _fft_split(seq)
    assert n1_len * n2_len == seq and n1_len % F_SUB == 0 and n2_len % F_SUB == 0
    assert seq % ATT_Q == 0 and seq >= ATT_KW and d == D_MODEL

    mod = _ada_mod(c, ada_w, ada_b).reshape(depth, n_batch, N_ADA, d)
    tabs = _rope_tables(positions)
    consts = _dft_constants(n1_len, n2_len)
    split = F_WIDTH + Q_WIDTH
    wr = jnp.pad(router_w, ((0, 0), (0, ROUTER_PAD - N_EXPERTS))).astype(BF16)
    fg = final_norm_g.reshape(1, d)

    x2 = x.reshape(n_batch * seq, d)
    for l in range(depth):
        w_l = w_in[l]
        w_bf = jnp.concatenate(
            [w_l[:, :split], w_l[:, split + 2 * KV_WIDTH:], w_l[:, split:split + 2 * KV_WIDTH]],
            axis=1).astype(BF16)
        u = _in_proj(x2, mod[l], norm1_g[l].reshape(1, d), w_bf, seq)
        yf = _fourier(u, n_batch, seq, consts)
        q_rot, k_rot = _rope(u, tabs)
        att = _attention(q_rot, k_rot, u, attn_sink[l], n_batch, seq)
        x1, h2, logits = _mix_out(
            yf, att, u, x2, mod[l], norm2_g[l].reshape(1, d),
            w_fourier_out[l].astype(BF16), w_attn_out[l].astype(BF16), w_out[l].astype(BF16),
            wr, seq)
        x2 = _moe(h2, logits, x1, mod[l], fg, router_b,
                  expert_w1[l].astype(BF16), expert_w3[l].astype(BF16),
                  expert_w2[l].astype(BF16), seq, final=(l == depth - 1))
    return x2.reshape(n_batch, seq, d)
```

```python
import functools
import math

import numpy as np
import jax
import jax.numpy as jnp
from jax import lax
from jax.experimental import pallas as pl
from jax.experimental.pallas import tpu as pltpu

F32 = jnp.float32
BF16 = jnp.bfloat16
I32 = jnp.int32

D_MODEL = 2048
N_ADA = 6
F_GROUPS = 4
F_GDIM = 256
F_WIDTH = F_GROUPS * F_GDIM
N_HEADS = 16
N_KV = 2
HEAD_DIM = 64
HEADS_PER_KV = N_HEADS // N_KV
Q_WIDTH = N_HEADS * HEAD_DIM
KV_WIDTH = N_KV * HEAD_DIM
ROPE_DIM = HEAD_DIM // 4
ROPE_HALF = ROPE_DIM // 2
ROPE_THETA = 500000.0
WINDOW = 128
N_EXPERTS = 16
N_GROUPS = 4
EPG = N_EXPERTS // N_GROUPS
TOP_K = 2
D_EXPERT = 768
MOE_BLOCK = 512
EPS = 1e-6
NEG_INF = -1e30
IN_WIDTH = F_WIDTH + Q_WIDTH + 2 * KV_WIDTH + 2 * D_MODEL

COL_GF = (F_WIDTH + Q_WIDTH) // D_MODEL
COL_GA = COL_GF + 1
COL_K = (F_WIDTH + Q_WIDTH + 2 * D_MODEL) // KV_WIDTH
COL_V = COL_K + 1

LANES = 128
SUBLANES = 8
BF16_ROWS = 16
VMEM_LIMIT = 56 * 1024 * 1024

ATT_Q = 256
ATT_KW = ATT_Q + 2 * WINDOW


def _cparams(sem, vmem=VMEM_LIMIT):
    return pltpu.CompilerParams(dimension_semantics=sem, vmem_limit_bytes=vmem)


def _tile(n, pref):
    t = min(n, pref)
    while n % t:
        t //= 2
    return t


ADA_TN = 768
ADA_ROWS = 32


def _ada_kernel(cb_ref, w_ref, b_ref, o_ref, *, n_batch):
    tn = w_ref.shape[-1]
    n_chunks = w_ref.shape[1] // ADA_ROWS

    def body(i, accs):
        r = pl.multiple_of(i * ADA_ROWS, ADA_ROWS)
        w = w_ref[0, pl.ds(r, ADA_ROWS), :].reshape(ADA_ROWS // SUBLANES, SUBLANES, tn)
        out = []
        for b in range(n_batch):
            c = cb_ref[b, pl.ds(r, ADA_ROWS), :]
            ca = c * jax.nn.sigmoid(c)
            cr = jnp.tile(ca, (1, tn // LANES)).reshape(ADA_ROWS // SUBLANES, SUBLANES, tn)
            out.append(accs[b] + jnp.sum(w * cr, axis=0))
        return tuple(out)

    accs = lax.fori_loop(0, n_chunks, body,
                         tuple(jnp.zeros((SUBLANES, tn), F32) for _ in range(n_batch)))
    for b in range(n_batch):
        o_ref[0, b:b + 1, :] = jnp.sum(accs[b], axis=0, keepdims=True) + b_ref[0]


def _ada_mod(c, ada_w, ada_b):
    n_layers, d, width = ada_w.shape
    n_batch = c.shape[0]
    cb = jnp.broadcast_to(c[:, :, None], (n_batch, d, LANES))
    return pl.pallas_call(
        functools.partial(_ada_kernel, n_batch=n_batch),
        grid=(n_layers, width // ADA_TN),
        in_specs=[
            pl.BlockSpec((n_batch, d, LANES), lambda l, j: (0, 0, 0)),
            pl.BlockSpec((1, d, ADA_TN), lambda l, j: (l, 0, j)),
            pl.BlockSpec((1, 1, ADA_TN), lambda l, j: (l, 0, j)),
        ],
        out_specs=pl.BlockSpec((1, n_batch, ADA_TN), lambda l, j: (l, 0, j)),
        out_shape=jax.ShapeDtypeStruct((n_layers, n_batch, width), F32),
        compiler_params=_cparams(("arbitrary", "arbitrary")),
        name="ada_mod",
    )(cb, ada_w, ada_b.reshape(n_layers, 1, width))


NORM_ROWS = 16


def _norm_mod_rows(x, g, shift, scale):
    ms = jnp.mean(x * x, axis=-1, keepdims=True)
    y = x * lax.rsqrt(ms + EPS) * g
    return y * (1.0 + scale) + shift


A_TM = 1024
A_TN = 1280


def _a_kernel(h_ref, w_ref, o_ref):
    o_ref[...] = jnp.dot(h_ref[...], w_ref[...], preferred_element_type=F32).astype(o_ref.dtype)


def _in_proj(h, w_bf):
    n, d = h.shape
    width = w_bf.shape[1]
    tm = _tile(n, A_TM)
    tn = _tile(width, A_TN)
    return pl.pallas_call(
        _a_kernel,
        grid=(n // tm, width // tn),
        in_specs=[
            pl.BlockSpec((tm, d), lambda i, j: (i, 0)),
            pl.BlockSpec((d, tn), lambda i, j: (0, j)),
        ],
        out_specs=pl.BlockSpec((tm, tn), lambda i, j: (i, j)),
        out_shape=jax.ShapeDtypeStruct((n, width), BF16),
        compiler_params=_cparams(("arbitrary", "arbitrary")),
        name="in_proj",
    )(h, w_bf)


N_TM = 512


def _norm_rows_to(x_ref, dst_ref, g, shift, scale):
    def body(i, carry):
        r = pl.multiple_of(i * NORM_ROWS, NORM_ROWS)
        h = _norm_mod_rows(x_ref[pl.ds(r, NORM_ROWS), :], g, shift, scale)
        dst_ref[pl.ds(r, NORM_ROWS), :] = h.astype(dst_ref.dtype)
        return carry

    lax.fori_loop(0, x_ref.shape[0] // NORM_ROWS, body, 0)


def _norm_kernel(x_ref, mod_ref, g_ref, h_ref):
    _norm_rows_to(x_ref, h_ref, g_ref[...], mod_ref[0, 0:1, :], mod_ref[0, 1:2, :])


def _entry_norm(x2, mod3, g, seq):
    n, d = x2.shape
    tm = _tile(seq, N_TM)
    per_batch = seq // tm
    return pl.pallas_call(
        _norm_kernel,
        grid=(n // tm,),
        in_specs=[
            pl.BlockSpec((tm, d), lambda i: (i, 0)),
            pl.BlockSpec((1, N_ADA, d), lambda i: (i // per_batch, 0, 0)),
            pl.BlockSpec((1, d), lambda i: (0, 0)),
        ],
        out_specs=pl.BlockSpec((tm, d), lambda i: (i, 0)),
        out_shape=jax.ShapeDtypeStruct((n, d), BF16),
        compiler_params=_cparams(("arbitrary",)),
        name="entry_norm",
    )(x2, mod3, g)


ROPE_TM = 512


def _rope_kernel(q_ref, k_ref, c_ref, sa_ref, sb_ref, qo_ref, ko_ref):
    c = c_ref[...]
    sa = sa_ref[...]
    sb = sb_ref[...]

    def rope(t):
        return (t * c + pltpu.roll(t, LANES - ROPE_HALF, 1) * sa
                + pltpu.roll(t, ROPE_HALF, 1) * sb)

    ko_ref[...] = rope(k_ref[...].astype(F32)).astype(BF16)
    for m in range(Q_WIDTH // LANES):
        q = q_ref[:, m * LANES:(m + 1) * LANES].astype(F32)
        qo_ref[:, m * LANES:(m + 1) * LANES] = (rope(q) * (HEAD_DIM ** -0.5)).astype(BF16)


def _rope(u, tabs):
    n = u.shape[0]
    tm = _tile(n, ROPE_TM)
    tab_spec = pl.BlockSpec((tm, LANES), lambda i: (i, 0))
    return pl.pallas_call(
        _rope_kernel,
        grid=(n // tm,),
        in_specs=[
            pl.BlockSpec((tm, Q_WIDTH), lambda i: (i, F_WIDTH // Q_WIDTH)),
            pl.BlockSpec((tm, KV_WIDTH), lambda i: (i, COL_K)),
            tab_spec, tab_spec, tab_spec,
        ],
        out_specs=[
            pl.BlockSpec((tm, Q_WIDTH), lambda i: (i, 0)),
            pl.BlockSpec((tm, KV_WIDTH), lambda i: (i, 0)),
        ],
        out_shape=[jax.ShapeDtypeStruct((n, Q_WIDTH), BF16),
                   jax.ShapeDtypeStruct((n, KV_WIDTH), BF16)],
        compiler_params=_cparams(("arbitrary",)),
        name="rope",
    )(u, u, *tabs)


def _rope_tables(positions):
    inv = ROPE_THETA ** (-jnp.arange(0, ROPE_DIM, 2, dtype=F32) / ROPE_DIM)
    ang = positions.astype(F32)[..., None] * inv
    cos = jnp.cos(ang).reshape(-1, ROPE_HALF)
    sin = jnp.sin(ang).reshape(-1, ROPE_HALF)
    n = cos.shape[0]
    rest = HEAD_DIM - ROPE_DIM
    ones = jnp.ones((n, rest), F32)
    z_half = jnp.zeros((n, ROPE_HALF), F32)
    z_rest = jnp.zeros((n, rest), F32)
    reps = LANES // HEAD_DIM
    c = jnp.tile(jnp.concatenate([cos, cos, ones], axis=1), (1, reps))
    sa = jnp.tile(jnp.concatenate([-sin, z_half, z_rest], axis=1), (1, reps))
    sb = jnp.tile(jnp.concatenate([z_half, sin, z_rest], axis=1), (1, reps))
    return c, sa, sb


F_SUB = BF16_ROWS


def _dft_constants(n1_len, n2_len):
    cidx = np.arange(F_GDIM)
    ang = 2.0 * np.pi * np.outer(cidx, cidx) / F_GDIM
    wc = np.concatenate([np.cos(ang), -np.sin(ang)], axis=1) / math.sqrt(F_GDIM)
    a = np.arange(n1_len)
    ang1 = 2.0 * np.pi * np.outer(a, a) / n1_len
    fr, fi = np.cos(ang1), -np.sin(ang1)
    fbig = np.block([[fr, -fi], [fi, fr]]) / math.sqrt(n1_len)
    k1 = np.arange(n1_len)[:, None, None]
    k2 = np.arange(n2_len)[None, :, None]
    b = np.arange(n2_len)[None, None, :]
    ang2 = 2.0 * np.pi * (b * k2 / n2_len + b * k1 / (n1_len * n2_len))
    g = np.concatenate([np.cos(ang2), np.sin(ang2)], axis=2) / math.sqrt(n2_len)
    return (jnp.asarray(wc, BF16), jnp.asarray(fbig, BF16), jnp.asarray(g, BF16))


def _f12_kernel(u_ref, wc_ref, fb_ref, t_ref):
    n1_len = u_ref.shape[1]
    x = u_ref[0].reshape(n1_len * F_SUB, F_WIDTH)
    zr, zi = [], []
    for g in range(F_GROUPS):
        r = jnp.dot(x[:, g * F_GDIM:(g + 1) * F_GDIM], wc_ref[...], preferred_element_type=F32)
        zr.append(r[:, :F_GDIM].astype(BF16))
        zi.append(r[:, F_GDIM:].astype(BF16))
    zr = jnp.concatenate(zr, axis=1).reshape(n1_len, F_SUB, F_WIDTH)
    zi = jnp.concatenate(zi, axis=1).reshape(n1_len, F_SUB, F_WIDTH)
    zr = pltpu.einshape("mnc->nmc", zr)
    zi = pltpu.einshape("mnc->nmc", zi)
    for j in range(F_SUB):
        rhs = jnp.concatenate([zr[j], zi[j]], axis=0)
        t = jnp.dot(fb_ref[...], rhs, preferred_element_type=F32)
        t_ref[0, j] = t.astype(BF16).reshape(2, n1_len, F_WIDTH)


def _f3_kernel(t_ref, g_ref, y_ref):
    tr = pltpu.einshape("nkc->knc", t_ref[0, :, 0])
    ti = pltpu.einshape("nkc->knc", t_ref[0, :, 1])
    ys = []
    for j in range(F_SUB):
        rhs = jnp.concatenate([tr[j], ti[j]], axis=0)
        ys.append(jnp.dot(g_ref[j], rhs, preferred_element_type=F32).astype(BF16))
    y_ref[0] = pltpu.einshape("knc->nkc", jnp.stack(ys, axis=0))


def _fourier(u, n_batch, seq, consts):
    wc, fbig, gmat = consts
    n1_len = fbig.shape[0] // 2
    n2_len = seq // n1_len
    u5 = u.reshape(n_batch, n1_len, n2_len, u.shape[-1])
    t = pl.pallas_call(
        _f12_kernel,
        grid=(n_batch, n2_len // F_SUB),
        in_specs=[
            pl.BlockSpec((1, n1_len, F_SUB, F_WIDTH), lambda b, j: (b, 0, j, 0)),
            pl.BlockSpec(wc.shape, lambda b, j: (0, 0)),
            pl.BlockSpec(fbig.shape, lambda b, j: (0, 0)),
        ],
        out_specs=pl.BlockSpec((1, F_SUB, 2, n1_len, F_WIDTH), lambda b, j: (b, j, 0, 0, 0)),
        out_shape=jax.ShapeDtypeStruct((n_batch, n2_len, 2, n1_len, F_WIDTH), BF16),
        compiler_params=_cparams(("arbitrary", "arbitrary")),
        name="fourier_stage1",
    )(u5, wc, fbig)
    y = pl.pallas_call(
        _f3_kernel,
        grid=(n_batch, n1_len // F_SUB),
        in_specs=[
            pl.BlockSpec((1, n2_len, 2, F_SUB, F_WIDTH), lambda b, j: (b, 0, 0, j, 0)),
            pl.BlockSpec((F_SUB, n2_len, 2 * n2_len), lambda b, j: (j, 0, 0)),
        ],
        out_specs=pl.BlockSpec((1, n2_len, F_SUB, F_WIDTH), lambda b, j: (b, 0, j, 0)),
        out_shape=jax.ShapeDtypeStruct((n_batch, n2_len, n1_len, F_WIDTH), BF16),
        compiler_params=_cparams(("arbitrary", "arbitrary")),
        name="fourier_stage2",
    )(t, gmat)
    return y.reshape(n_batch * seq, F_WIDTH)


def _att_kernel(sink_ref, q_ref, k_ref, v_ref, o_ref, *, seq):
    i = pl.program_id(1)
    s0 = i * ATT_Q
    ws = pl.multiple_of(jnp.clip(s0 - WINDOW, 0, seq - ATT_KW), WINDOW)
    kw = k_ref[0, pl.ds(ws, ATT_KW), :].astype(F32)
    vw = v_ref[0, pl.ds(ws, ATT_KW), :].astype(F32)
    lane = lax.broadcasted_iota(I32, (ATT_KW, LANES), 1)
    low = lane < HEAD_DIM
    kw_sw = pltpu.roll(kw, HEAD_DIM, 1)
    vw_sw = pltpu.roll(vw, HEAD_DIM, 1)
    row = lax.broadcasted_iota(I32, (ATT_Q, ATT_KW), 0)
    col = lax.broadcasted_iota(I32, (ATT_Q, ATT_KW), 1)
    rel = (ws + col) - (s0 + row)
    bias = jnp.where(jnp.abs(rel) <= WINDOW, 0.0, NEG_INF).astype(F32)
    for h in range(N_KV):
        k_own, k_oth = (kw, kw_sw) if h == 0 else (kw_sw, kw)
        v_own, v_oth = (vw, vw_sw) if h == 0 else (vw_sw, vw)
        k_lo = jnp.where(low, k_own, 0.0).astype(BF16)
        k_hi = jnp.where(low, 0.0, k_oth).astype(BF16)
        v_lo = jnp.where(low, v_own, 0.0).astype(BF16)
        v_hi = jnp.where(low, 0.0, v_oth).astype(BF16)
        for m in range(HEADS_PER_KV // 2):
            cb = (h * HEADS_PER_KV // 2 + m) * LANES
            q2 = q_ref[:, cb:cb + LANES]
            acc = jnp.zeros((ATT_Q, LANES), F32)
            for half, (kk, vv) in enumerate(((k_lo, v_lo), (k_hi, v_hi))):
                sink = sink_ref[h * HEADS_PER_KV + 2 * m + half]
                s = lax.dot_general(q2, kk, (((1,), (1,)), ((), ())),
                                    preferred_element_type=F32) + bias
                mx = jnp.maximum(jnp.max(s, axis=-1, keepdims=True), sink)
                p = jnp.exp(s - mx)
                den = jnp.sum(p, axis=-1, keepdims=True) + jnp.exp(sink - mx)
                o = jnp.dot(p.astype(BF16), vv, preferred_element_type=F32)
                acc = acc + o / den
            o_ref[:, cb:cb + LANES] = acc.astype(BF16)


def _attention(q_rot, k_rot, u, sink, n_batch, seq):
    n = q_rot.shape[0]
    nq = seq // ATT_Q
    k3 = k_rot.reshape(n_batch, seq, KV_WIDTH)
    u3 = u.reshape(n_batch, seq, u.shape[-1])
    return pl.pallas_call(
        functools.partial(_att_kernel, seq=seq),
        grid=(n_batch, nq),
        in_specs=[
            pl.BlockSpec(memory_space=pltpu.SMEM),
            pl.BlockSpec((ATT_Q, Q_WIDTH), lambda b, i: (b * nq + i, 0)),
            pl.BlockSpec((1, seq, KV_WIDTH), lambda b, i: (b, 0, 0)),
            pl.BlockSpec((1, seq, KV_WIDTH), lambda b, i: (b, 0, COL_V)),
        ],
        out_specs=pl.BlockSpec((ATT_Q, Q_WIDTH), lambda b, i: (b * nq + i, 0)),
        out_shape=jax.ShapeDtypeStruct((n, Q_WIDTH), BF16),
        compiler_params=_cparams(("arbitrary", "arbitrary")),
        name="window_attention",
    )(sink, q_rot, k3, u3)


D_TM = 512
ROUTER_PAD = LANES


def _d_kernel(f_ref, a_ref, gf_ref, ga_ref, x_ref, mod_ref, wfo_ref, wao_ref, wout_ref, x1_ref):
    yf = jnp.dot(f_ref[...], wfo_ref[...], preferred_element_type=F32)
    ya = jnp.dot(a_ref[...], wao_ref[...], preferred_element_type=F32)
    merged = (jax.nn.sigmoid(gf_ref[...].astype(F32)) * yf
              + jax.nn.sigmoid(ga_ref[...].astype(F32)) * ya)
    out = jnp.dot(merged.astype(BF16), wout_ref[...], preferred_element_type=F32)
    x1_ref[...] = x_ref[...] + mod_ref[0, 2:3, :] * out


def _const_spec(shape):
    nd = len(shape)
    return pl.BlockSpec(shape, lambda i: (0,) * nd, pipeline_mode=pl.Buffered(1))


def _mix_out(yf, att, u, x2, mod3, wfo, wao, wout, seq):
    n, d = x2.shape
    tm = _tile(seq, D_TM)
    per_batch = seq // tm
    row = lambda w: pl.BlockSpec((tm, w), lambda i: (i, 0))
    return pl.pallas_call(
        _d_kernel,
        grid=(n // tm,),
        in_specs=[
            row(F_WIDTH), row(Q_WIDTH),
            pl.BlockSpec((tm, d), lambda i: (i, COL_GF)),
            pl.BlockSpec((tm, d), lambda i: (i, COL_GA)),
            row(d),
            pl.BlockSpec((1, N_ADA, d), lambda i: (i // per_batch, 0, 0)),
            _const_spec(wfo.shape), _const_spec(wao.shape), _const_spec(wout.shape),
        ],
        out_specs=row(d),
        out_shape=jax.ShapeDtypeStruct((n, d), F32),
        compiler_params=_cparams(("arbitrary",)),
        name="mix_out",
    )(yf, att, u, u, x2, mod3, wfo, wao, wout)


R_TN = 1024
R_SUB = 512
U32 = jnp.uint32
HI_MASK = 0xFFFF0000


def _pack_bf16_pairs(hb):
    bits = lax.bitcast_convert_type(hb.astype(F32), U32)
    half = hb.shape[1] // 2
    return (bits[:, half:] & U32(HI_MASK)) | (bits[:, :half] >> U32(16))


def _unpack_bf16_pairs(xu):
    lo = lax.bitcast_convert_type(xu << U32(16), F32).astype(BF16)
    hi = lax.bitcast_convert_type(xu & U32(HI_MASK), F32).astype(BF16)
    return jnp.concatenate([lo, hi], axis=1)


def _route_kernel(bias_ref, x_ref, mod_ref, g2_ref, wr_ref, tri_ref,
                  hp_ref, e_ref, w_ref, p_ref, cnt_ref, carry_ref, hb_ref):
    @pl.when(pl.program_id(0) == 0)
    def _():
        carry_ref[...] = jnp.zeros_like(carry_ref)

    tn = x_ref.shape[0]
    g2 = g2_ref[...]
    shift = mod_ref[0, 3:4, :]
    scale = mod_ref[0, 4:5, :]

    def norm_body(i, c):
        r = pl.multiple_of(i * NORM_ROWS, NORM_ROWS)
        hb = _norm_mod_rows(x_ref[pl.ds(r, NORM_ROWS), :], g2, shift, scale).astype(BF16)
        hb_ref[pl.ds(r, NORM_ROWS), :] = hb
        hp_ref[pl.ds(r, NORM_ROWS), :] = _pack_bf16_pairs(hb)
        return c

    lax.fori_loop(0, tn // NORM_ROWS, norm_body, 0)
    logits = jnp.dot(hb_ref[...], wr_ref[...], preferred_element_type=F32)
    lt = logits.T
    sc = [jax.nn.sigmoid(lt[e:e + 1, :]) for e in range(N_EXPERTS)]
    bz = [sc[e] + bias_ref[e] for e in range(N_EXPERTS)]
    one = jnp.ones((1, tn), F32)
    zero = jnp.zeros((1, tn), F32)
    in_top, gscore = [], []
    for g in range(N_GROUPS):
        v = bz[g * EPG:(g + 1) * EPG]
        rank = [zero] * EPG
        for a in range(EPG):
            for b in range(a + 1, EPG):
                a_wins = jnp.where(v[a] >= v[b], one, zero)
                rank[b] = rank[b] + a_wins
                rank[a] = rank[a] + (one - a_wins)
        top = [jnp.where(rank[a] < TOP_K - 0.5, one, zero) for a in range(EPG)]
        in_top += top
        gs = zero
        for a in range(EPG):
            gs = gs + top[a] * v[a]
        gscore.append(gs)
    sel = []
    for g in range(N_GROUPS):
        ok = one
        for other in range(N_GROUPS):
            if other < g:
                ok = ok * jnp.where(gscore[g] > gscore[other], one, zero)
            elif other > g:
                ok = ok * jnp.where(gscore[g] >= gscore[other], one, zero)
        sel += [in_top[g * EPG + a] * ok for a in range(EPG)]

    selm = jnp.concatenate(sel, axis=0)
    carry = carry_ref[:, 0:1]
    cums = []
    for c in range(tn // R_SUB):
        blk = selm[:, c * R_SUB:(c + 1) * R_SUB]
        cums.append(jnp.dot(blk.astype(BF16), tri_ref[...], preferred_element_type=F32) + carry)
        carry = carry + jnp.sum(blk, axis=1, keepdims=True)
    cum = jnp.concatenate(cums, axis=1)
    carry_ref[...] = jnp.broadcast_to(carry, carry_ref.shape)
    cnt_ref[...] = jnp.broadcast_to(carry, cnt_ref.shape)

    before = zero
    eidx = [zero, zero]
    score = [zero, zero]
    pos = [zero, zero]
    for e in range(N_EXPERTS):
        pick = [sel[e] * jnp.where(before < 0.5, one, zero),
                sel[e] * jnp.where(before >= 0.5, one, zero)]
        for k in range(TOP_K):
            eidx[k] = eidx[k] + pick[k] * float(e)
            score[k] = score[k] + pick[k] * sc[e]
            pos[k] = pos[k] + pick[k] * cum[e:e + 1, :]
        before = before + sel[e]
    tot = score[0] + score[1]
    for k in range(TOP_K):
        e_ref[k:k + 1, :] = eidx[k].astype(I32)
        w_ref[k:k + 1, :] = score[k] / tot
        p_ref[k:k + 1, :] = pos[k].astype(I32)


def _route(x1, mod3, g2, wr, router_b, seq):
    n, d = x1.shape
    tn = _tile(seq, R_TN)
    per_batch = seq // tn
    tri = jnp.asarray(np.triu(np.ones((R_SUB, R_SUB), np.float32), k=1), BF16)
    kn = lambda: pl.BlockSpec((TOP_K, tn), lambda i: (0, i))
    return pl.pallas_call(
        _route_kernel,
        grid=(n // tn,),
        in_specs=[
            pl.BlockSpec(memory_space=pltpu.SMEM),
            pl.BlockSpec((tn, d), lambda i: (i, 0)),
            pl.BlockSpec((1, N_ADA, d), lambda i: (i // per_batch, 0, 0)),
            pl.BlockSpec((1, d), lambda i: (0, 0)),
            pl.BlockSpec(wr.shape, lambda i: (0, 0)),
            pl.BlockSpec((R_SUB, R_SUB), lambda i: (0, 0)),
        ],
        out_specs=[pl.BlockSpec((tn, d // 2), lambda i: (i, 0)), kn(), kn(), kn(),
                   pl.BlockSpec((N_EXPERTS, LANES), lambda i: (0, 0))],
        out_shape=[jax.ShapeDtypeStruct((n, d // 2), U32),
                   jax.ShapeDtypeStruct((TOP_K, n), I32), jax.ShapeDtypeStruct((TOP_K, n), F32),
                   jax.ShapeDtypeStruct((TOP_K, n), I32),
                   jax.ShapeDtypeStruct((N_EXPERTS, LANES), F32)],
        scratch_shapes=[pltpu.VMEM((N_EXPERTS, LANES), F32), pltpu.VMEM((tn, d), BF16)],
        compiler_params=_cparams(("arbitrary",)),
        name="route",
    )(router_b, x1, mod3, g2, wr, tri)


P_TM = 256
P_ZROWS = 256


def _row_copy(src, src_row, dst, dst_row, sem):
    return pltpu.make_async_copy(src.at[pl.ds(src_row, 1), :], dst.at[pl.ds(dst_row, 1), :], sem)


def _dispatch_kernel(pstart_ref, padlo_ref, padhi_ref, e_ref, p_ref, h_ref, xs_ref,
                     zrow_ref, sem, zsem):
    tm = h_ref.shape[0]

    def issue(r, carry):
        for k in range(TOP_K):
            slot = pstart_ref[e_ref[k, r]] + p_ref[k, r]
            _row_copy(h_ref, r, xs_ref, slot, sem).start(priority=k)
        return carry

    lax.fori_loop(0, tm, issue, 0)

    @pl.when(pl.program_id(0) == pl.num_programs(0) - 1)
    def _():
        zrow_ref[...] = jnp.zeros_like(zrow_ref)
        zrows = zrow_ref.shape[0]

        def zissue(r, carry):
            _row_copy(zrow_ref, 0, xs_ref, r, zsem).start()
            return carry

        def zwait(r, carry):
            _row_copy(zrow_ref, 0, xs_ref, r, zsem).wait()
            return carry

        for e in range(N_EXPERTS):
            lax.fori_loop(padlo_ref[e], padhi_ref[e], zissue, 0)
            lax.fori_loop(padlo_ref[e], padhi_ref[e], zwait, 0)

        def chunk(j):
            start = pl.multiple_of(padhi_ref[N_EXPERTS - 1] + j * zrows, zrows)
            return pltpu.make_async_copy(zrow_ref, xs_ref.at[pl.ds(start, zrows), :], zsem)

        n_chunks = (xs_ref.shape[0] - padhi_ref[N_EXPERTS - 1]) // zrows
        lax.fori_loop(0, n_chunks, lambda j, c: (chunk(j).start(), c)[1], 0)
        lax.fori_loop(0, n_chunks, lambda j, c: (chunk(j).wait(), c)[1], 0)

    for k in range(TOP_K):
        pltpu.make_async_copy(h_ref, xs_ref.at[pl.ds(0, tm), :], sem).wait()


def _slot_specs(tm):
    return [pl.BlockSpec((TOP_K, tm), lambda i, *_: (0, i), memory_space=pltpu.SMEM)
            for _ in range(2)]


def _dispatch(hp, eidx, pos, pstart, pad_lo, pad_hi, cap):
    n, dp = hp.shape
    tm = _tile(n, P_TM)
    grid_spec = pltpu.PrefetchScalarGridSpec(
        num_scalar_prefetch=3,
        grid=(n // tm,),
        in_specs=_slot_specs(tm) + [pl.BlockSpec((tm, dp), lambda i, *_: (i, 0))],
        out_specs=pl.BlockSpec(memory_space=pl.ANY),
        scratch_shapes=[pltpu.VMEM((P_ZROWS, dp), hp.dtype), pltpu.SemaphoreType.DMA(()),
                        pltpu.SemaphoreType.DMA(())],
    )
    return pl.pallas_call(
        _dispatch_kernel,
        grid_spec=grid_spec,
        out_shape=jax.ShapeDtypeStruct((cap, dp), hp.dtype),
        compiler_params=_cparams(("arbitrary",)),
        name="dispatch",
    )(pstart, pad_lo, pad_hi, eidx, pos, hp)


def _expert_kernel(be_ref, nu_ref, xs_ref, w1_ref, w3_ref, w2_ref, ys_ref):
    @pl.when(pl.program_id(0) < nu_ref[0])
    def _():
        x = _unpack_bf16_pairs(xs_ref[...])
        a = jnp.dot(x, w1_ref[0], preferred_element_type=F32)
        b = jnp.dot(x, w3_ref[0], preferred_element_type=F32)
        h = (a * jax.nn.sigmoid(a)) * b
        ys_ref[...] = jnp.dot(h.astype(BF16), w2_ref[0], preferred_element_type=F32)

    @pl.when(pl.program_id(0) >= nu_ref[0])
    def _():
        ys_ref[...] = jnp.zeros_like(ys_ref)


def _experts(xs, block_e, n_used, w1, w3, w2):
    cap, dp = xs.shape
    n_blocks = cap // MOE_BLOCK
    d, de = w1.shape[1:]
    blk = lambda i, be, nu: (jnp.minimum(i, nu[0] - 1), 0)
    wsel = lambda i, be, nu: (be[jnp.minimum(i, nu[0] - 1)], 0, 0)
    grid_spec = pltpu.PrefetchScalarGridSpec(
        num_scalar_prefetch=2,
        grid=(n_blocks,),
        in_specs=[
            pl.BlockSpec((MOE_BLOCK, dp), blk),
            pl.BlockSpec((1, d, de), wsel),
            pl.BlockSpec((1, d, de), wsel),
            pl.BlockSpec((1, de, d), wsel),
        ],
        out_specs=pl.BlockSpec((MOE_BLOCK, d), lambda i, be, nu: (i, 0)),
    )
    return pl.pallas_call(
        _expert_kernel,
        grid_spec=grid_spec,
        out_shape=jax.ShapeDtypeStruct((cap, d), F32),
        compiler_params=_cparams(("arbitrary",)),
        name="experts",
    )(block_e, n_used, xs, w1, w3, w2)


C_TM = 256


def _combine_kernel(pstart_ref, e_ref, p_ref, x_ref, wb_ref, mod_ref, modn_ref, gn_ref, ys_ref,
                    *rest, final):
    if final:
        o_ref, buf_ref, sem = rest
    else:
        o_ref, hn_ref, buf_ref, sem = rest
    tm = x_ref.shape[0]

    def issue(r, carry):
        for k in range(TOP_K):
            slot = pstart_ref[e_ref[k, r]] + p_ref[k, r]
            _row_copy(ys_ref, slot, buf_ref.at[k], r, sem).start(priority=k)
        return carry

    lax.fori_loop(0, tm, issue, 0)
    for k in range(TOP_K):
        pltpu.make_async_copy(ys_ref.at[pl.ds(0, tm), :], buf_ref.at[k], sem).wait()

    reps = x_ref.shape[1] // LANES
    y = (jnp.tile(wb_ref[0], (1, reps)) * buf_ref[0]
         + jnp.tile(wb_ref[1], (1, reps)) * buf_ref[1])
    out = x_ref[...] + mod_ref[0, 5:6, :] * y
    if final:
        ms = jnp.mean(out * out, axis=-1, keepdims=True)
        o_ref[...] = out * lax.rsqrt(ms + EPS) * gn_ref[...]
    else:
        o_ref[...] = out
        _norm_rows_to(o_ref, hn_ref, gn_ref[...], modn_ref[0, 0:1, :], modn_ref[0, 1:2, :])


def _combine(x1, ys, eidx, pos, pstart, wb, mod3, modn3, gn, seq, final):
    n, d = x1.shape
    tm = _tile(seq, C_TM)
    per_batch = seq // tm
    row = pl.BlockSpec((tm, d), lambda i, *_: (i, 0))
    mod_spec = pl.BlockSpec((1, N_ADA, d), lambda i, *_: (i // per_batch, 0, 0))
    grid_spec = pltpu.PrefetchScalarGridSpec(
        num_scalar_prefetch=1,
        grid=(n // tm,),
        in_specs=_slot_specs(tm) + [
            row,
            pl.BlockSpec((TOP_K, tm, LANES), lambda i, *_: (0, i, 0)),
            mod_spec, mod_spec,
            pl.BlockSpec((1, d), lambda i, *_: (0, 0)),
            pl.BlockSpec(memory_space=pl.ANY),
        ],
        out_specs=row if final else [row, row],
        scratch_shapes=[pltpu.VMEM((TOP_K, tm, d), F32), pltpu.SemaphoreType.DMA(())],
    )
    x_sds = jax.ShapeDtypeStruct((n, d), F32)
    return pl.pallas_call(
        functools.partial(_combine_kernel, final=final),
        grid_spec=grid_spec,
        out_shape=x_sds if final else [x_sds, jax.ShapeDtypeStruct((n, d), BF16)],
        compiler_params=_cparams(("arbitrary",)),
        name="combine",
    )(pstart, eidx, pos, x1, wb, mod3, modn3, gn, ys)


def _moe(x1, mod3, g2, modn3, gn, wr, router_b, w1, w3, w2, seq, final):
    n, d = x1.shape
    hp, eidx, gw, pos, cnt = _route(x1, mod3, g2, wr, router_b, seq)
    counts = cnt[:, 0].astype(I32)
    padded = (counts + MOE_BLOCK - 1) // MOE_BLOCK * MOE_BLOCK
    pends = jnp.cumsum(padded)
    pstarts = (pends - padded).astype(I32)
    n_blocks = -(-(n * TOP_K) // MOE_BLOCK) + N_EXPERTS
    block_lo = jnp.arange(n_blocks, dtype=I32) * MOE_BLOCK
    block_e = jnp.minimum(jnp.sum((pends[None, :] <= block_lo[:, None]).astype(I32), axis=1),
                          N_EXPERTS - 1).astype(I32)
    n_used = (pends[-1:] // MOE_BLOCK).astype(I32)
    xs = _dispatch(hp, eidx, pos, pstarts, (pstarts + counts).astype(I32), pends.astype(I32),
                   n_blocks * MOE_BLOCK)
    ys = _experts(xs, block_e, n_used, w1, w3, w2)
    wb = jnp.broadcast_to(gw[:, :, None], (TOP_K, n, LANES))
    return _combine(x1, ys, eidx, pos, pstarts, wb, mod3, modn3, gn, seq, final)


def _fft_split(seq):
    n1_len = 1 << (int(math.log2(seq)) // 2)
    return n1_len, seq // n1_len


def kernel(x, c, positions, ada_w, ada_b, norm1_g, norm2_g, w_in, w_fourier_out, w_attn_out,
           attn_sink, w_out, router_w, router_b, expert_w1, expert_w3, expert_w2, final_norm_g):
    n_batch, seq, d = x.shape
    depth = ada_w.shape[0]
    n1_len, n2_len = _fft_split(seq)
    assert n1_len * n2_len == seq and n1_len % F_SUB == 0 and n2_len % F_SUB == 0
    assert seq % ATT_Q == 0 and seq >= ATT_KW and d == D_MODEL

    mod = _ada_mod(c, ada_w, ada_b).reshape(depth, n_batch, N_ADA, d)
    tabs = _rope_tables(positions)
    consts = _dft_constants(n1_len, n2_len)
    split = F_WIDTH + Q_WIDTH
    wr = jnp.pad(router_w, ((0, 0), (0, ROUTER_PAD - N_EXPERTS))).astype(BF16)

    x2 = x.reshape(n_batch * seq, d)
    h = _entry_norm(x2, mod[0], norm1_g[0].reshape(1, d), seq)
    for l in range(depth):
        final = l == depth - 1
        w_l = w_in[l]
        w_bf = jnp.concatenate(
            [w_l[:, :split], w_l[:, split + 2 * KV_WIDTH:], w_l[:, split:split + 2 * KV_WIDTH]],
            axis=1).astype(BF16)
        u = _in_proj(h, w_bf)
        yf = _fourier(u, n_batch, seq, consts)
        q_rot, k_rot = _rope(u, tabs)
        att = _attention(q_rot, k_rot, u, attn_sink[l], n_batch, seq)
        x1 = _mix_out(yf, att, u, x2, mod[l], w_fourier_out[l].astype(BF16),
                      w_attn_out[l].astype(BF16), w_out[l].astype(BF16), seq)
        gn = (final_norm_g if final else norm1_g[l + 1]).reshape(1, d)
        res = _moe(x1, mod[l], norm2_g[l].reshape(1, d), mod[l if final else l + 1], gn, wr,
                   router_b, expert_w1[l].astype(BF16), expert_w3[l].astype(BF16),
                   expert_w2[l].astype(BF16), seq, final)
        if final:
            x2 = res
        else:
            x2, h = res
    return x2.reshape(n_batch, seq, d)
```

```python
import functools
import math

import numpy as np
import jax
import jax.numpy as jnp
from jax import lax
from jax.experimental import pallas as pl
from jax.experimental.pallas import tpu as pltpu

F32 = jnp.float32
BF16 = jnp.bfloat16
I32 = jnp.int32

D_MODEL = 2048
N_ADA = 6
F_GROUPS = 4
F_GDIM = 256
F_WIDTH = F_GROUPS * F_GDIM
N_HEADS = 16
N_KV = 2
HEAD_DIM = 64
HEADS_PER_KV = N_HEADS // N_KV
Q_WIDTH = N_HEADS * HEAD_DIM
KV_WIDTH = N_KV * HEAD_DIM
ROPE_DIM = HEAD_DIM // 4
ROPE_HALF = ROPE_DIM // 2
ROPE_THETA = 500000.0
WINDOW = 128
N_EXPERTS = 16
N_GROUPS = 4
EPG = N_EXPERTS // N_GROUPS
TOP_K = 2
D_EXPERT = 768
MOE_BLOCK = 512
EPS = 1e-6
NEG_INF = -1e30
IN_WIDTH = F_WIDTH + Q_WIDTH + 2 * KV_WIDTH + 2 * D_MODEL

COL_GF = (F_WIDTH + Q_WIDTH) // D_MODEL
COL_GA = COL_GF + 1
COL_K = (F_WIDTH + Q_WIDTH + 2 * D_MODEL) // KV_WIDTH
COL_V = COL_K + 1

LANES = 128
SUBLANES = 8
BF16_ROWS = 16
VMEM_LIMIT = 56 * 1024 * 1024

ATT_Q = 256
ATT_KW = ATT_Q + 2 * WINDOW
ATT_SUB = 128
ATT_SW = ATT_SUB + 2 * WINDOW
ATT_GROUP = 4


def _cparams(sem, vmem=VMEM_LIMIT):
    return pltpu.CompilerParams(dimension_semantics=sem, vmem_limit_bytes=vmem)


def _tile(n, pref):
    t = min(n, pref)
    while n % t:
        t //= 2
    return t


ADA_TN = 768
ADA_ROWS = 32


def _ada_kernel(cb_ref, w_ref, b_ref, o_ref, *, n_batch):
    tn = w_ref.shape[-1]
    n_chunks = w_ref.shape[1] // ADA_ROWS

    def body(i, accs):
        r = pl.multiple_of(i * ADA_ROWS, ADA_ROWS)
        w = w_ref[0, pl.ds(r, ADA_ROWS), :].reshape(ADA_ROWS // SUBLANES, SUBLANES, tn)
        out = []
        for b in range(n_batch):
            c = cb_ref[b, pl.ds(r, ADA_ROWS), :]
            ca = c * jax.nn.sigmoid(c)
            cr = jnp.tile(ca, (1, tn // LANES)).reshape(ADA_ROWS // SUBLANES, SUBLANES, tn)
            out.append(accs[b] + jnp.sum(w * cr, axis=0))
        return tuple(out)

    accs = lax.fori_loop(0, n_chunks, body,
                         tuple(jnp.zeros((SUBLANES, tn), F32) for _ in range(n_batch)))
    for b in range(n_batch):
        o_ref[0, b:b + 1, :] = jnp.sum(accs[b], axis=0, keepdims=True) + b_ref[0]


def _ada_mod(c, ada_w, ada_b):
    n_layers, d, width = ada_w.shape
    n_batch = c.shape[0]
    cb = jnp.broadcast_to(c[:, :, None], (n_batch, d, LANES))
    return pl.pallas_call(
        functools.partial(_ada_kernel, n_batch=n_batch),
        grid=(n_layers, width // ADA_TN),
        in_specs=[
            pl.BlockSpec((n_batch, d, LANES), lambda l, j: (0, 0, 0)),
            pl.BlockSpec((1, d, ADA_TN), lambda l, j: (l, 0, j)),
            pl.BlockSpec((1, 1, ADA_TN), lambda l, j: (l, 0, j)),
        ],
        out_specs=pl.BlockSpec((1, n_batch, ADA_TN), lambda l, j: (l, 0, j)),
        out_shape=jax.ShapeDtypeStruct((n_layers, n_batch, width), F32),
        compiler_params=_cparams(("arbitrary", "arbitrary")),
        name="ada_mod",
    )(cb, ada_w, ada_b.reshape(n_layers, 1, width))


NORM_ROWS = 16
NORM_UNROLL = 8


def _norm_mod_rows(x, g, shift, scale):
    ms = jnp.mean(x * x, axis=-1, keepdims=True)
    y = x * lax.rsqrt(ms + EPS) * g
    return y * (1.0 + scale) + shift


A_TM = 1024
A_TN = 1280


def _a_kernel(h_ref, w_ref, o_ref):
    o_ref[...] = jnp.dot(h_ref[...], w_ref[...], preferred_element_type=F32).astype(o_ref.dtype)


def _in_proj(h, w_bf):
    n, d = h.shape
    width = w_bf.shape[1]
    tm = _tile(n, A_TM)
    tn = _tile(width, A_TN)
    return pl.pallas_call(
        _a_kernel,
        grid=(n // tm, width // tn),
        in_specs=[
            pl.BlockSpec((tm, d), lambda i, j: (i, 0)),
            pl.BlockSpec((d, tn), lambda i, j: (0, j)),
        ],
        out_specs=pl.BlockSpec((tm, tn), lambda i, j: (i, j)),
        out_shape=jax.ShapeDtypeStruct((n, width), BF16),
        compiler_params=_cparams(("arbitrary", "arbitrary")),
        name="in_proj",
    )(h, w_bf)


N_TM = 512


def _norm_rows_to(x_ref, dst_ref, g, shift, scale):
    def body(i, carry):
        r = pl.multiple_of(i * NORM_ROWS, NORM_ROWS)
        h = _norm_mod_rows(x_ref[pl.ds(r, NORM_ROWS), :], g, shift, scale)
        dst_ref[pl.ds(r, NORM_ROWS), :] = h.astype(dst_ref.dtype)
        return carry

    lax.fori_loop(0, x_ref.shape[0] // NORM_ROWS, body, 0, unroll=NORM_UNROLL)


def _norm_kernel(x_ref, mod_ref, g_ref, h_ref):
    _norm_rows_to(x_ref, h_ref, g_ref[...], mod_ref[0, 0:1, :], mod_ref[0, 1:2, :])


def _entry_norm(x2, mod3, g, seq):
    n, d = x2.shape
    tm = _tile(seq, N_TM)
    per_batch = seq // tm
    return pl.pallas_call(
        _norm_kernel,
        grid=(n // tm,),
        in_specs=[
            pl.BlockSpec((tm, d), lambda i: (i, 0)),
            pl.BlockSpec((1, N_ADA, d), lambda i: (i // per_batch, 0, 0)),
            pl.BlockSpec((1, d), lambda i: (0, 0)),
        ],
        out_specs=pl.BlockSpec((tm, d), lambda i: (i, 0)),
        out_shape=jax.ShapeDtypeStruct((n, d), BF16),
        compiler_params=_cparams(("arbitrary",)),
        name="entry_norm",
    )(x2, mod3, g)


ROPE_TM = 512


def _rope_kernel(q_ref, k_ref, c_ref, sa_ref, sb_ref, qo_ref, ko_ref):
    c = c_ref[...]
    sa = sa_ref[...]
    sb = sb_ref[...]

    def rope(t):
        return (t * c + pltpu.roll(t, LANES - ROPE_HALF, 1) * sa
                + pltpu.roll(t, ROPE_HALF, 1) * sb)

    ko_ref[...] = rope(k_ref[...].astype(F32)).astype(BF16)
    for m in range(Q_WIDTH // LANES):
        q = q_ref[:, m * LANES:(m + 1) * LANES].astype(F32)
        qo_ref[:, m * LANES:(m + 1) * LANES] = (rope(q) * (HEAD_DIM ** -0.5)).astype(BF16)


def _rope(u, tabs):
    n = u.shape[0]
    tm = _tile(n, ROPE_TM)
    tab_spec = pl.BlockSpec((tm, LANES), lambda i: (i, 0))
    return pl.pallas_call(
        _rope_kernel,
        grid=(n // tm,),
        in_specs=[
            pl.BlockSpec((tm, Q_WIDTH), lambda i: (i, F_WIDTH // Q_WIDTH)),
            pl.BlockSpec((tm, KV_WIDTH), lambda i: (i, COL_K)),
            tab_spec, tab_spec, tab_spec,
        ],
        out_specs=[
            pl.BlockSpec((tm, Q_WIDTH), lambda i: (i, 0)),
            pl.BlockSpec((tm, KV_WIDTH), lambda i: (i, 0)),
        ],
        out_shape=[jax.ShapeDtypeStruct((n, Q_WIDTH), BF16),
                   jax.ShapeDtypeStruct((n, KV_WIDTH), BF16)],
        compiler_params=_cparams(("arbitrary",)),
        name="rope",
    )(u, u, *tabs)


def _rope_tables(positions):
    inv = ROPE_THETA ** (-jnp.arange(0, ROPE_DIM, 2, dtype=F32) / ROPE_DIM)
    ang = positions.astype(F32)[..., None] * inv
    cos = jnp.cos(ang).reshape(-1, ROPE_HALF)
    sin = jnp.sin(ang).reshape(-1, ROPE_HALF)
    n = cos.shape[0]
    rest = HEAD_DIM - ROPE_DIM
    ones = jnp.ones((n, rest), F32)
    z_half = jnp.zeros((n, ROPE_HALF), F32)
    z_rest = jnp.zeros((n, rest), F32)
    reps = LANES // HEAD_DIM
    c = jnp.tile(jnp.concatenate([cos, cos, ones], axis=1), (1, reps))
    sa = jnp.tile(jnp.concatenate([-sin, z_half, z_rest], axis=1), (1, reps))
    sb = jnp.tile(jnp.concatenate([z_half, sin, z_rest], axis=1), (1, reps))
    return c, sa, sb


F_SUB = BF16_ROWS


def _dft_constants(n1_len, n2_len):
    cidx = np.arange(F_GDIM)
    ang = 2.0 * np.pi * np.outer(cidx, cidx) / F_GDIM
    wc = np.concatenate([np.cos(ang), -np.sin(ang)], axis=1) / math.sqrt(F_GDIM)
    a = np.arange(n1_len)
    ang1 = 2.0 * np.pi * np.outer(a, a) / n1_len
    fr, fi = np.cos(ang1), -np.sin(ang1)
    fbig = np.block([[fr, -fi], [fi, fr]]) / math.sqrt(n1_len)
    k1 = np.arange(n1_len)[:, None, None]
    k2 = np.arange(n2_len)[None, :, None]
    b = np.arange(n2_len)[None, None, :]
    ang2 = 2.0 * np.pi * (b * k2 / n2_len + b * k1 / (n1_len * n2_len))
    g = np.concatenate([np.cos(ang2), np.sin(ang2)], axis=2) / math.sqrt(n2_len)
    return (jnp.asarray(wc, BF16), jnp.asarray(fbig, BF16), jnp.asarray(g, BF16))


def _f12_kernel(u_ref, wc_ref, fb_ref, t_ref):
    n1_len = u_ref.shape[1]
    x = u_ref[0].reshape(n1_len * F_SUB, F_WIDTH)
    zr, zi = [], []
    for g in range(F_GROUPS):
        r = jnp.dot(x[:, g * F_GDIM:(g + 1) * F_GDIM], wc_ref[...], preferred_element_type=F32)
        zr.append(r[:, :F_GDIM].astype(BF16))
        zi.append(r[:, F_GDIM:].astype(BF16))
    zr = jnp.concatenate(zr, axis=1).reshape(n1_len, F_SUB, F_WIDTH)
    zi = jnp.concatenate(zi, axis=1).reshape(n1_len, F_SUB, F_WIDTH)
    zr = pltpu.einshape("mnc->nmc", zr)
    zi = pltpu.einshape("mnc->nmc", zi)
    for j in range(F_SUB):
        rhs = jnp.concatenate([zr[j], zi[j]], axis=0)
        t = jnp.dot(fb_ref[...], rhs, preferred_element_type=F32)
        t_ref[0, j] = t.astype(BF16).reshape(2, n1_len, F_WIDTH)


def _f3_kernel(t_ref, g_ref, y_ref):
    tr = pltpu.einshape("nkc->knc", t_ref[0, :, 0])
    ti = pltpu.einshape("nkc->knc", t_ref[0, :, 1])
    ys = []
    for j in range(F_SUB):
        rhs = jnp.concatenate([tr[j], ti[j]], axis=0)
        ys.append(jnp.dot(g_ref[j], rhs, preferred_element_type=F32).astype(BF16))
    y_ref[0] = pltpu.einshape("knc->nkc", jnp.stack(ys, axis=0))


def _fourier(u, n_batch, seq, consts):
    wc, fbig, gmat = consts
    n1_len = fbig.shape[0] // 2
    n2_len = seq // n1_len
    u5 = u.reshape(n_batch, n1_len, n2_len, u.shape[-1])
    t = pl.pallas_call(
        _f12_kernel,
        grid=(n_batch, n2_len // F_SUB),
        in_specs=[
            pl.BlockSpec((1, n1_len, F_SUB, F_WIDTH), lambda b, j: (b, 0, j, 0)),
            pl.BlockSpec(wc.shape, lambda b, j: (0, 0)),
            pl.BlockSpec(fbig.shape, lambda b, j: (0, 0)),
        ],
        out_specs=pl.BlockSpec((1, F_SUB, 2, n1_len, F_WIDTH), lambda b, j: (b, j, 0, 0, 0)),
        out_shape=jax.ShapeDtypeStruct((n_batch, n2_len, 2, n1_len, F_WIDTH), BF16),
        compiler_params=_cparams(("arbitrary", "arbitrary")),
        name="fourier_stage1",
    )(u5, wc, fbig)
    y = pl.pallas_call(
        _f3_kernel,
        grid=(n_batch, n1_len // F_SUB),
        in_specs=[
            pl.BlockSpec((1, n2_len, 2, F_SUB, F_WIDTH), lambda b, j: (b, 0, 0, j, 0)),
            pl.BlockSpec((F_SUB, n2_len, 2 * n2_len), lambda b, j: (j, 0, 0)),
        ],
        out_specs=pl.BlockSpec((1, n2_len, F_SUB, F_WIDTH), lambda b, j: (b, 0, j, 0)),
        out_shape=jax.ShapeDtypeStruct((n_batch, n2_len, n1_len, F_WIDTH), BF16),
        compiler_params=_cparams(("arbitrary", "arbitrary")),
        name="fourier_stage2",
    )(t, gmat)
    return y.reshape(n_batch * seq, F_WIDTH)


def _att_kernel(sink_ref, q_ref, k_ref, v_ref, o_ref, kv_ref, *, seq):
    i = pl.program_id(1)
    s0 = i * ATT_Q
    ws = pl.multiple_of(jnp.clip(s0 - WINDOW, 0, seq - ATT_KW), WINDOW)
    off = s0 - ws
    lane = lax.broadcasted_iota(I32, (ATT_KW, LANES), 1)
    low = lane < HEAD_DIM
    for t, ref in enumerate((k_ref, v_ref)):
        w = ref[0, pl.ds(ws, ATT_KW), :].astype(F32)
        w_sw = pltpu.roll(w, HEAD_DIM, 1)
        for h in range(N_KV):
            own, oth = (w, w_sw) if h == 0 else (w_sw, w)
            kv_ref[t, h, 0] = jnp.where(low, own, 0.0).astype(BF16)
            kv_ref[t, h, 1] = jnp.where(low, 0.0, oth).astype(BF16)

    def sub_block(a, carry):
        r0 = pl.multiple_of(a * ATT_SUB, ATT_SUB)
        c0 = pl.multiple_of(
            jnp.clip(((off + r0) // WINDOW) * WINDOW - WINDOW, 0, ATT_KW - ATT_SW), WINDOW)
        row = lax.broadcasted_iota(I32, (ATT_SUB, ATT_SW), 0)
        col = lax.broadcasted_iota(I32, (ATT_SUB, ATT_SW), 1)
        rel = (c0 + col) - (off + r0 + row)
        bias = jnp.where(jnp.abs(rel) <= WINDOW, 0.0, NEG_INF).astype(F32)
        pairs = [(h, m) for h in range(N_KV) for m in range(HEADS_PER_KV // 2)]
        for g0 in range(0, len(pairs), ATT_GROUP):
            group = [(h, m, half) for (h, m) in pairs[g0:g0 + ATT_GROUP] for half in range(2)]
            scores = []
            for h, m, half in group:
                cb = (h * HEADS_PER_KV // 2 + m) * LANES
                q2 = q_ref[pl.ds(r0, ATT_SUB), cb:cb + LANES]
                kk = kv_ref[0, h, half, pl.ds(c0, ATT_SW), :]
                scores.append(lax.dot_general(q2, kk, (((1,), (1,)), ((), ())),
                                              preferred_element_type=F32))
            probs, rdens = [], []
            for (h, m, half), s in zip(group, scores):
                sink = sink_ref[h * HEADS_PER_KV + 2 * m + half]
                s = s + bias
                mx = jnp.maximum(jnp.max(s, axis=-1, keepdims=True), sink)
                p = jnp.exp(s - mx)
                rdens.append(1.0 / (jnp.sum(p, axis=-1, keepdims=True) + jnp.exp(sink - mx)))
                probs.append(p.astype(BF16))
            outs = []
            for (h, m, half), p, rden in zip(group, probs, rdens):
                vv = kv_ref[1, h, half, pl.ds(c0, ATT_SW), :]
                outs.append(jnp.dot(p, vv, preferred_element_type=F32) * rden)
            for j, (h, m) in enumerate(pairs[g0:g0 + ATT_GROUP]):
                cb = (h * HEADS_PER_KV // 2 + m) * LANES
                o_ref[pl.ds(r0, ATT_SUB), cb:cb + LANES] = (outs[2 * j] + outs[2 * j + 1]).astype(BF16)
        return carry

    lax.fori_loop(0, ATT_Q // ATT_SUB, sub_block, 0)


def _attention(q_rot, k_rot, u, sink, n_batch, seq):
    n = q_rot.shape[0]
    nq = seq // ATT_Q
    k3 = k_rot.reshape(n_batch, seq, KV_WIDTH)
    u3 = u.reshape(n_batch, seq, u.shape[-1])
    return pl.pallas_call(
        functools.partial(_att_kernel, seq=seq),
        grid=(n_batch, nq),
        in_specs=[
            pl.BlockSpec(memory_space=pltpu.SMEM),
            pl.BlockSpec((ATT_Q, Q_WIDTH), lambda b, i: (b * nq + i, 0)),
            pl.BlockSpec((1, seq, KV_WIDTH), lambda b, i: (b, 0, 0)),
            pl.BlockSpec((1, seq, KV_WIDTH), lambda b, i: (b, 0, COL_V)),
        ],
        out_specs=pl.BlockSpec((ATT_Q, Q_WIDTH), lambda b, i: (b * nq + i, 0)),
        out_shape=jax.ShapeDtypeStruct((n, Q_WIDTH), BF16),
        scratch_shapes=[pltpu.VMEM((2, N_KV, 2, ATT_KW, LANES), BF16)],
        compiler_params=_cparams(("arbitrary", "arbitrary")),
        name="window_attention",
    )(sink, q_rot, k3, u3)


D_TM = 512
ROUTER_PAD = LANES


def _d_kernel(f_ref, a_ref, gf_ref, ga_ref, x_ref, mod_ref, wfo_ref, wao_ref, wout_ref, x1_ref):
    yf = jnp.dot(f_ref[...], wfo_ref[...], preferred_element_type=F32)
    ya = jnp.dot(a_ref[...], wao_ref[...], preferred_element_type=F32)
    merged = (jax.nn.sigmoid(gf_ref[...].astype(F32)) * yf
              + jax.nn.sigmoid(ga_ref[...].astype(F32)) * ya)
    out = jnp.dot(merged.astype(BF16), wout_ref[...], preferred_element_type=F32)
    x1_ref[...] = x_ref[...] + mod_ref[0, 2:3, :] * out


def _const_spec(shape):
    nd = len(shape)
    return pl.BlockSpec(shape, lambda i: (0,) * nd, pipeline_mode=pl.Buffered(1))


def _mix_out(yf, att, u, x2, mod3, wfo, wao, wout, seq):
    n, d = x2.shape
    tm = _tile(seq, D_TM)
    per_batch = seq // tm
    row = lambda w: pl.BlockSpec((tm, w), lambda i: (i, 0))
    return pl.pallas_call(
        _d_kernel,
        grid=(n // tm,),
        in_specs=[
            row(F_WIDTH), row(Q_WIDTH),
            pl.BlockSpec((tm, d), lambda i: (i, COL_GF)),
            pl.BlockSpec((tm, d), lambda i: (i, COL_GA)),
            row(d),
            pl.BlockSpec((1, N_ADA, d), lambda i: (i // per_batch, 0, 0)),
            _const_spec(wfo.shape), _const_spec(wao.shape), _const_spec(wout.shape),
        ],
        out_specs=row(d),
        out_shape=jax.ShapeDtypeStruct((n, d), F32),
        compiler_params=_cparams(("arbitrary",)),
        name="mix_out",
    )(yf, att, u, u, x2, mod3, wfo, wao, wout)


R_TN = 1024
R_SUB = 512
U32 = jnp.uint32
HI_MASK = 0xFFFF0000


def _pack_bf16_pairs(hb):
    bits = lax.bitcast_convert_type(hb.astype(F32), U32)
    half = hb.shape[1] // 2
    return (bits[:, half:] & U32(HI_MASK)) | (bits[:, :half] >> U32(16))


def _unpack_bf16_pairs(xu):
    lo = lax.bitcast_convert_type(xu << U32(16), F32).astype(BF16)
    hi = lax.bitcast_convert_type(xu & U32(HI_MASK), F32).astype(BF16)
    return jnp.concatenate([lo, hi], axis=1)


def _route_kernel(bias_ref, x_ref, mod_ref, g2_ref, wr_ref, tri_ref,
                  hp_ref, e_ref, w_ref, p_ref, cnt_ref, carry_ref, hb_ref):
    @pl.when(pl.program_id(0) == 0)
    def _():
        carry_ref[...] = jnp.zeros_like(carry_ref)

    tn = x_ref.shape[0]
    g2 = g2_ref[...]
    shift = mod_ref[0, 3:4, :]
    scale = mod_ref[0, 4:5, :]

    def norm_body(i, c):
        r = pl.multiple_of(i * NORM_ROWS, NORM_ROWS)
        hb = _norm_mod_rows(x_ref[pl.ds(r, NORM_ROWS), :], g2, shift, scale).astype(BF16)
        hb_ref[pl.ds(r, NORM_ROWS), :] = hb
        hp_ref[pl.ds(r, NORM_ROWS), :] = _pack_bf16_pairs(hb)
        return c

    lax.fori_loop(0, tn // NORM_ROWS, norm_body, 0, unroll=NORM_UNROLL)
    logits = jnp.dot(hb_ref[...], wr_ref[...], preferred_element_type=F32)
    lt = logits.T
    sc = [jax.nn.sigmoid(lt[e:e + 1, :]) for e in range(N_EXPERTS)]
    bz = [sc[e] + bias_ref[e] for e in range(N_EXPERTS)]
    one = jnp.ones((1, tn), F32)
    zero = jnp.zeros((1, tn), F32)
    in_top, gscore = [], []
    for g in range(N_GROUPS):
        v = bz[g * EPG:(g + 1) * EPG]
        rank = [zero] * EPG
        for a in range(EPG):
            for b in range(a + 1, EPG):
                a_wins = jnp.where(v[a] >= v[b], one, zero)
                rank[b] = rank[b] + a_wins
                rank[a] = rank[a] + (one - a_wins)
        top = [jnp.where(rank[a] < TOP_K - 0.5, one, zero) for a in range(EPG)]
        in_top += top
        gs = zero
        for a in range(EPG):
            gs = gs + top[a] * v[a]
        gscore.append(gs)
    sel = []
    for g in range(N_GROUPS):
        ok = one
        for other in range(N_GROUPS):
            if other < g:
                ok = ok * jnp.where(gscore[g] > gscore[other], one, zero)
            elif other > g:
                ok = ok * jnp.where(gscore[g] >= gscore[other], one, zero)
        sel += [in_top[g * EPG + a] * ok for a in range(EPG)]

    selm = jnp.concatenate(sel, axis=0)
    carry = carry_ref[:, 0:1]
    cums = []
    for c in range(tn // R_SUB):
        blk = selm[:, c * R_SUB:(c + 1) * R_SUB]
        cums.append(jnp.dot(blk.astype(BF16), tri_ref[...], preferred_element_type=F32) + carry)
        carry = carry + jnp.sum(blk, axis=1, keepdims=True)
    cum = jnp.concatenate(cums, axis=1)
    carry_ref[...] = jnp.broadcast_to(carry, carry_ref.shape)
    cnt_ref[...] = jnp.broadcast_to(carry, cnt_ref.shape)

    before = zero
    eidx = [zero, zero]
    score = [zero, zero]
    pos = [zero, zero]
    for e in range(N_EXPERTS):
        pick = [sel[e] * jnp.where(before < 0.5, one, zero),
                sel[e] * jnp.where(before >= 0.5, one, zero)]
        for k in range(TOP_K):
            eidx[k] = eidx[k] + pick[k] * float(e)
            score[k] = score[k] + pick[k] * sc[e]
            pos[k] = pos[k] + pick[k] * cum[e:e + 1, :]
        before = before + sel[e]
    tot = score[0] + score[1]
    for k in range(TOP_K):
        e_ref[k:k + 1, :] = eidx[k].astype(I32)
        w_ref[k:k + 1, :] = score[k] / tot
        p_ref[k:k + 1, :] = pos[k].astype(I32)


def _route(x1, mod3, g2, wr, router_b, seq):
    n, d = x1.shape
    tn = _tile(seq, R_TN)
    per_batch = seq // tn
    tri = jnp.asarray(np.triu(np.ones((R_SUB, R_SUB), np.float32), k=1), BF16)
    kn = lambda: pl.BlockSpec((TOP_K, tn), lambda i: (0, i))
    return pl.pallas_call(
        _route_kernel,
        grid=(n // tn,),
        in_specs=[
            pl.BlockSpec(memory_space=pltpu.SMEM),
            pl.BlockSpec((tn, d), lambda i: (i, 0)),
            pl.BlockSpec((1, N_ADA, d), lambda i: (i // per_batch, 0, 0)),
            pl.BlockSpec((1, d), lambda i: (0, 0)),
            pl.BlockSpec(wr.shape, lambda i: (0, 0)),
            pl.BlockSpec((R_SUB, R_SUB), lambda i: (0, 0)),
        ],
        out_specs=[pl.BlockSpec((tn, d // 2), lambda i: (i, 0)), kn(), kn(), kn(),
                   pl.BlockSpec((N_EXPERTS, LANES), lambda i: (0, 0))],
        out_shape=[jax.ShapeDtypeStruct((n, d // 2), U32),
                   jax.ShapeDtypeStruct((TOP_K, n), I32), jax.ShapeDtypeStruct((TOP_K, n), F32),
                   jax.ShapeDtypeStruct((TOP_K, n), I32),
                   jax.ShapeDtypeStruct((N_EXPERTS, LANES), F32)],
        scratch_shapes=[pltpu.VMEM((N_EXPERTS, LANES), F32), pltpu.VMEM((tn, d), BF16)],
        compiler_params=_cparams(("arbitrary",)),
        name="route",
    )(router_b, x1, mod3, g2, wr, tri)


P_TM = 256
P_ZROWS = 256
ISSUE_UNROLL = 8


def _row_copy(src, src_row, dst, dst_row, sem):
    return pltpu.make_async_copy(src.at[pl.ds(src_row, 1), :], dst.at[pl.ds(dst_row, 1), :], sem)


def _dispatch_kernel(padlo_ref, padhi_ref, slot_ref, h_ref, xs_ref, zrow_ref, sem, zsem):
    tm = h_ref.shape[0]

    def issue(i, carry):
        for j in range(ISSUE_UNROLL):
            r = i * ISSUE_UNROLL + j
            for k in range(TOP_K):
                _row_copy(h_ref, r, xs_ref, slot_ref[k, r], sem).start(priority=k)
        return carry

    lax.fori_loop(0, tm // ISSUE_UNROLL, issue, 0)

    @pl.when(pl.program_id(0) == pl.num_programs(0) - 1)
    def _():
        zrow_ref[...] = jnp.zeros_like(zrow_ref)
        zrows = zrow_ref.shape[0]

        def zissue(r, carry):
            _row_copy(zrow_ref, 0, xs_ref, r, zsem).start()
            return carry

        def zwait(r, carry):
            _row_copy(zrow_ref, 0, xs_ref, r, zsem).wait()
            return carry

        for e in range(N_EXPERTS):
            lax.fori_loop(padlo_ref[e], padhi_ref[e], zissue, 0)
            lax.fori_loop(padlo_ref[e], padhi_ref[e], zwait, 0)

        def chunk(j):
            start = pl.multiple_of(padhi_ref[N_EXPERTS - 1] + j * zrows, zrows)
            return pltpu.make_async_copy(zrow_ref, xs_ref.at[pl.ds(start, zrows), :], zsem)

        n_chunks = (xs_ref.shape[0] - padhi_ref[N_EXPERTS - 1]) // zrows
        lax.fori_loop(0, n_chunks, lambda j, c: (chunk(j).start(), c)[1], 0)
        lax.fori_loop(0, n_chunks, lambda j, c: (chunk(j).wait(), c)[1], 0)

    for k in range(TOP_K):
        pltpu.make_async_copy(h_ref, xs_ref.at[pl.ds(0, tm), :], sem).wait()


def _slot_spec(tm):
    return pl.BlockSpec((TOP_K, tm), lambda i, *_: (0, i), memory_space=pltpu.SMEM)


def _dispatch(hp, slot, pad_lo, pad_hi, cap):
    n, dp = hp.shape
    tm = _tile(n, P_TM)
    grid_spec = pltpu.PrefetchScalarGridSpec(
        num_scalar_prefetch=2,
        grid=(n // tm,),
        in_specs=[_slot_spec(tm), pl.BlockSpec((tm, dp), lambda i, *_: (i, 0))],
        out_specs=pl.BlockSpec(memory_space=pl.ANY),
        scratch_shapes=[pltpu.VMEM((P_ZROWS, dp), hp.dtype), pltpu.SemaphoreType.DMA(()),
                        pltpu.SemaphoreType.DMA(())],
    )
    return pl.pallas_call(
        _dispatch_kernel,
        grid_spec=grid_spec,
        out_shape=jax.ShapeDtypeStruct((cap, dp), hp.dtype),
        compiler_params=_cparams(("arbitrary",)),
        name="dispatch",
    )(pad_lo, pad_hi, slot, hp)


def _expert_kernel(be_ref, nu_ref, xs_ref, w1_ref, w3_ref, w2_ref, ys_ref):
    @pl.when(pl.program_id(0) < nu_ref[0])
    def _():
        x = _unpack_bf16_pairs(xs_ref[...])
        a = jnp.dot(x, w1_ref[0], preferred_element_type=F32)
        b = jnp.dot(x, w3_ref[0], preferred_element_type=F32)
        h = (a * jax.nn.sigmoid(a)) * b
        ys_ref[...] = jnp.dot(h.astype(BF16), w2_ref[0], preferred_element_type=F32)

    @pl.when(pl.program_id(0) >= nu_ref[0])
    def _():
        ys_ref[...] = jnp.zeros_like(ys_ref)


def _experts(xs, block_e, n_used, w1, w3, w2):
    cap, dp = xs.shape
    n_blocks = cap // MOE_BLOCK
    d, de = w1.shape[1:]
    blk = lambda i, be, nu: (jnp.minimum(i, nu[0] - 1), 0)
    wsel = lambda i, be, nu: (be[jnp.minimum(i, nu[0] - 1)], 0, 0)
    grid_spec = pltpu.PrefetchScalarGridSpec(
        num_scalar_prefetch=2,
        grid=(n_blocks,),
        in_specs=[
            pl.BlockSpec((MOE_BLOCK, dp), blk),
            pl.BlockSpec((1, d, de), wsel),
            pl.BlockSpec((1, d, de), wsel),
            pl.BlockSpec((1, de, d), wsel),
        ],
        out_specs=pl.BlockSpec((MOE_BLOCK, d), lambda i, be, nu: (i, 0)),
    )
    return pl.pallas_call(
        _expert_kernel,
        grid_spec=grid_spec,
        out_shape=jax.ShapeDtypeStruct((cap, d), F32),
        compiler_params=_cparams(("arbitrary",)),
        name="experts",
    )(block_e, n_used, xs, w1, w3, w2)


C_TM = 256


def _combine_kernel(slot_ref, x_ref, wb_ref, mod_ref, modn_ref, gn_ref, ys_ref, *rest, final):
    if final:
        o_ref, buf_ref, sem = rest
    else:
        o_ref, hn_ref, buf_ref, sem = rest
    tm = x_ref.shape[0]

    def issue(i, carry):
        for j in range(ISSUE_UNROLL):
            r = i * ISSUE_UNROLL + j
            for k in range(TOP_K):
                _row_copy(ys_ref, slot_ref[k, r], buf_ref.at[k], r, sem).start(priority=k)
        return carry

    lax.fori_loop(0, tm // ISSUE_UNROLL, issue, 0)
    for k in range(TOP_K):
        pltpu.make_async_copy(ys_ref.at[pl.ds(0, tm), :], buf_ref.at[k], sem).wait()

    reps = x_ref.shape[1] // LANES
    y = (jnp.tile(wb_ref[0], (1, reps)) * buf_ref[0]
         + jnp.tile(wb_ref[1], (1, reps)) * buf_ref[1])
    out = x_ref[...] + mod_ref[0, 5:6, :] * y
    if final:
        ms = jnp.mean(out * out, axis=-1, keepdims=True)
        o_ref[...] = out * lax.rsqrt(ms + EPS) * gn_ref[...]
    else:
        o_ref[...] = out
        _norm_rows_to(o_ref, hn_ref, gn_ref[...], modn_ref[0, 0:1, :], modn_ref[0, 1:2, :])


def _combine(x1, ys, slot, wb, mod3, modn3, gn, seq, final):
    n, d = x1.shape
    tm = _tile(seq, C_TM)
    per_batch = seq // tm
    row = pl.BlockSpec((tm, d), lambda i, *_: (i, 0))
    mod_spec = pl.BlockSpec((1, N_ADA, d), lambda i, *_: (i // per_batch, 0, 0))
    grid_spec = pltpu.PrefetchScalarGridSpec(
        num_scalar_prefetch=0,
        grid=(n // tm,),
        in_specs=[
            _slot_spec(tm),
            row,
            pl.BlockSpec((TOP_K, tm, LANES), lambda i, *_: (0, i, 0)),
            mod_spec, mod_spec,
            pl.BlockSpec((1, d), lambda i, *_: (0, 0)),
            pl.BlockSpec(memory_space=pl.ANY),
        ],
        out_specs=row if final else [row, row],
        scratch_shapes=[pltpu.VMEM((TOP_K, tm, d), F32), pltpu.SemaphoreType.DMA(())],
    )
    x_sds = jax.ShapeDtypeStruct((n, d), F32)
    return pl.pallas_call(
        functools.partial(_combine_kernel, final=final),
        grid_spec=grid_spec,
        out_shape=x_sds if final else [x_sds, jax.ShapeDtypeStruct((n, d), BF16)],
        compiler_params=_cparams(("arbitrary",)),
        name="combine",
    )(slot, x1, wb, mod3, modn3, gn, ys)


def _moe(x1, mod3, g2, modn3, gn, wr, router_b, w1, w3, w2, seq, final):
    n, d = x1.shape
    hp, eidx, gw, pos, cnt = _route(x1, mod3, g2, wr, router_b, seq)
    counts = cnt[:, 0].astype(I32)
    padded = (counts + MOE_BLOCK - 1) // MOE_BLOCK * MOE_BLOCK
    pends = jnp.cumsum(padded)
    pstarts = (pends - padded).astype(I32)
    n_blocks = -(-(n * TOP_K) // MOE_BLOCK) + N_EXPERTS
    block_lo = jnp.arange(n_blocks, dtype=I32) * MOE_BLOCK
    block_e = jnp.minimum(jnp.sum((pends[None, :] <= block_lo[:, None]).astype(I32), axis=1),
                          N_EXPERTS - 1).astype(I32)
    n_used = (pends[-1:] // MOE_BLOCK).astype(I32)
    seg = jnp.sum(jnp.where(eidx[:, :, None] == jnp.arange(N_EXPERTS, dtype=I32), pstarts, 0),
                  axis=-1)
    slot = (seg + pos).astype(I32)
    xs = _dispatch(hp, slot, (pstarts + counts).astype(I32), pends.astype(I32),
                   n_blocks * MOE_BLOCK)
    ys = _experts(xs, block_e, n_used, w1, w3, w2)
    wb = jnp.broadcast_to(gw[:, :, None], (TOP_K, n, LANES))
    return _combine(x1, ys, slot, wb, mod3, modn3, gn, seq, final)


def _fft_split(seq):
    n1_len = 1 << (int(math.log2(seq)) // 2)
    return n1_len, seq // n1_len


def kernel(x, c, positions, ada_w, ada_b, norm1_g, norm2_g, w_in, w_fourier_out, w_attn_out,
           attn_sink, w_out, router_w, router_b, expert_w1, expert_w3, expert_w2, final_norm_g):
    n_batch, seq, d = x.shape
    depth = ada_w.shape[0]
    n1_len, n2_len = _fft_split(seq)
    assert n1_len * n2_len == seq and n1_len % F_SUB == 0 and n2_len % F_SUB == 0
    assert seq % ATT_Q == 0 and seq >= ATT_KW and d == D_MODEL

    mod = _ada_mod(c, ada_w, ada_b).reshape(depth, n_batch, N_ADA, d)
    tabs = _rope_tables(positions)
    consts = _dft_constants(n1_len, n2_len)
    split = F_WIDTH + Q_WIDTH
    wr = jnp.pad(router_w, ((0, 0), (0, ROUTER_PAD - N_EXPERTS))).astype(BF16)

    x2 = x.reshape(n_batch * seq, d)
    h = _entry_norm(x2, mod[0], norm1_g[0].reshape(1, d), seq)
    for l in range(depth):
        final = l == depth - 1
        w_l = w_in[l]
        w_bf = jnp.concatenate(
            [w_l[:, :split], w_l[:, split + 2 * KV_WIDTH:], w_l[:, split:split + 2 * KV_WIDTH]],
            axis=1).astype(BF16)
        u = _in_proj(h, w_bf)
        yf = _fourier(u, n_batch, seq, consts)
        q_rot, k_rot = _rope(u, tabs)
        att = _attention(q_rot, k_rot, u, attn_sink[l], n_batch, seq)
        x1 = _mix_out(yf, att, u, x2, mod[l], w_fourier_out[l].astype(BF16),
                      w_attn_out[l].astype(BF16), w_out[l].astype(BF16), seq)
        gn = (final_norm_g if final else norm1_g[l + 1]).reshape(1, d)
        res = _moe(x1, mod[l], norm2_g[l].reshape(1, d), mod[l if final else l + 1], gn, wr,
                   router_b, expert_w1[l].astype(BF16), expert_w3[l].astype(BF16),
                   expert_w2[l].astype(BF16), seq, final)
        if final:
            x2 = res
        else:
            x2, h = res
    return x2.reshape(n_batch, seq, d)
```

```python
import functools
import math

import numpy as np
import jax
import jax.numpy as jnp
from jax import lax
from jax.experimental import pallas as pl
from jax.experimental.pallas import tpu as pltpu

F32 = jnp.float32
BF16 = jnp.bfloat16
I32 = jnp.int32

D_MODEL = 2048
N_ADA = 6
F_GROUPS = 4
F_GDIM = 256
F_WIDTH = F_GROUPS * F_GDIM
N_HEADS = 16
N_KV = 2
HEAD_DIM = 64
HEADS_PER_KV = N_HEADS // N_KV
Q_WIDTH = N_HEADS * HEAD_DIM
KV_WIDTH = N_KV * HEAD_DIM
ROPE_DIM = HEAD_DIM // 4
ROPE_HALF = ROPE_DIM // 2
ROPE_THETA = 500000.0
WINDOW = 128
N_EXPERTS = 16
N_GROUPS = 4
EPG = N_EXPERTS // N_GROUPS
TOP_K = 2
D_EXPERT = 768
MOE_BLOCK = 512
EPS = 1e-6
NEG_INF = -1e30
IN_WIDTH = F_WIDTH + Q_WIDTH + 2 * KV_WIDTH + 2 * D_MODEL

COL_GF = (F_WIDTH + Q_WIDTH) // D_MODEL
COL_GA = COL_GF + 1
COL_K = (F_WIDTH + Q_WIDTH + 2 * D_MODEL) // KV_WIDTH
COL_V = COL_K + 1

LANES = 128
SUBLANES = 8
BF16_ROWS = 16
VMEM_LIMIT = 56 * 1024 * 1024

ATT_Q = 256
ATT_KW = ATT_Q + 2 * WINDOW
ATT_SUB = 128
ATT_SW = ATT_SUB + 2 * WINDOW
ATT_GROUP = 4


def _cparams(sem, vmem=VMEM_LIMIT):
    return pltpu.CompilerParams(dimension_semantics=sem, vmem_limit_bytes=vmem)


def _tile(n, pref):
    t = min(n, pref)
    while n % t:
        t //= 2
    return t


ADA_TN = 768
ADA_ROWS = 32


def _ada_kernel(cb_ref, w_ref, b_ref, o_ref, *, n_batch):
    tn = w_ref.shape[-1]
    n_chunks = w_ref.shape[1] // ADA_ROWS

    def body(i, accs):
        r = pl.multiple_of(i * ADA_ROWS, ADA_ROWS)
        w = w_ref[0, pl.ds(r, ADA_ROWS), :].reshape(ADA_ROWS // SUBLANES, SUBLANES, tn)
        out = []
        for b in range(n_batch):
            c = cb_ref[b, pl.ds(r, ADA_ROWS), :]
            ca = c * jax.nn.sigmoid(c)
            cr = jnp.tile(ca, (1, tn // LANES)).reshape(ADA_ROWS // SUBLANES, SUBLANES, tn)
            out.append(accs[b] + jnp.sum(w * cr, axis=0))
        return tuple(out)

    accs = lax.fori_loop(0, n_chunks, body,
                         tuple(jnp.zeros((SUBLANES, tn), F32) for _ in range(n_batch)))
    for b in range(n_batch):
        o_ref[0, b:b + 1, :] = jnp.sum(accs[b], axis=0, keepdims=True) + b_ref[0]


def _ada_mod(c, ada_w, ada_b):
    n_layers, d, width = ada_w.shape
    n_batch = c.shape[0]
    cb = jnp.broadcast_to(c[:, :, None], (n_batch, d, LANES))
    return pl.pallas_call(
        functools.partial(_ada_kernel, n_batch=n_batch),
        grid=(n_layers, width // ADA_TN),
        in_specs=[
            pl.BlockSpec((n_batch, d, LANES), lambda l, j: (0, 0, 0)),
            pl.BlockSpec((1, d, ADA_TN), lambda l, j: (l, 0, j)),
            pl.BlockSpec((1, 1, ADA_TN), lambda l, j: (l, 0, j)),
        ],
        out_specs=pl.BlockSpec((1, n_batch, ADA_TN), lambda l, j: (l, 0, j)),
        out_shape=jax.ShapeDtypeStruct((n_layers, n_batch, width), F32),
        compiler_params=_cparams(("arbitrary", "arbitrary")),
        name="ada_mod",
    )(cb, ada_w, ada_b.reshape(n_layers, 1, width))


NORM_ROWS = 16
NORM_UNROLL = 8


def _norm_mod_rows(x, g, shift, scale):
    ms = jnp.mean(x * x, axis=-1, keepdims=True)
    y = x * lax.rsqrt(ms + EPS) * g
    return y * (1.0 + scale) + shift


A_TM = 1024
A_TN = 1280


def _a_kernel(h_ref, w_ref, o_ref):
    o_ref[...] = jnp.dot(h_ref[...], w_ref[...], preferred_element_type=F32).astype(o_ref.dtype)


def _in_proj(h, w_bf, layer):
    n, d = h.shape
    width = w_bf.shape[-1]
    tm = _tile(n, A_TM)
    tn = _tile(width, A_TN)
    return pl.pallas_call(
        _a_kernel,
        grid=(n // tm, width // tn),
        in_specs=[
            pl.BlockSpec((tm, d), lambda i, j: (i, 0)),
            pl.BlockSpec((None, d, tn), lambda i, j: (layer, 0, j)),
        ],
        out_specs=pl.BlockSpec((tm, tn), lambda i, j: (i, j)),
        out_shape=jax.ShapeDtypeStruct((n, width), BF16),
        compiler_params=_cparams(("arbitrary", "arbitrary")),
        name="in_proj",
    )(h, w_bf)


W_TILE = 2 * KV_WIDTH


def _cast_kernel(w_ref, o_ref):
    o_ref[...] = w_ref[...].astype(o_ref.dtype)


def _reorder_cast_w_in(w_in):
    depth, d, width = w_in.shape
    kv_tile = (F_WIDTH + Q_WIDTH) // W_TILE
    n_tiles = width // W_TILE
    assert (F_WIDTH + Q_WIDTH) % W_TILE == 0 and width % W_TILE == 0

    def src_tile(j):
        return jnp.where(j < kv_tile, j, jnp.where(j < n_tiles - 1, j + 1, kv_tile))

    return pl.pallas_call(
        _cast_kernel,
        grid=(depth, n_tiles),
        in_specs=[pl.BlockSpec((1, d, W_TILE), lambda l, j: (l, 0, src_tile(j)))],
        out_specs=pl.BlockSpec((1, d, W_TILE), lambda l, j: (l, 0, j)),
        out_shape=jax.ShapeDtypeStruct((depth, d, width), BF16),
        compiler_params=_cparams(("arbitrary", "arbitrary")),
        name="w_in_cast",
    )(w_in)


N_TM = 512


def _norm_rows_to(x_ref, dst_ref, g, shift, scale):
    def body(i, carry):
        r = pl.multiple_of(i * NORM_ROWS, NORM_ROWS)
        h = _norm_mod_rows(x_ref[pl.ds(r, NORM_ROWS), :], g, shift, scale)
        dst_ref[pl.ds(r, NORM_ROWS), :] = h.astype(dst_ref.dtype)
        return carry

    lax.fori_loop(0, x_ref.shape[0] // NORM_ROWS, body, 0, unroll=NORM_UNROLL)


def _norm_kernel(x_ref, mod_ref, g_ref, h_ref):
    _norm_rows_to(x_ref, h_ref, g_ref[...], mod_ref[0, 0:1, :], mod_ref[0, 1:2, :])


def _entry_norm(x2, mod3, g, seq):
    n, d = x2.shape
    tm = _tile(seq, N_TM)
    per_batch = seq // tm
    return pl.pallas_call(
        _norm_kernel,
        grid=(n // tm,),
        in_specs=[
            pl.BlockSpec((tm, d), lambda i: (i, 0)),
            pl.BlockSpec((1, N_ADA, d), lambda i: (i // per_batch, 0, 0)),
            pl.BlockSpec((1, d), lambda i: (0, 0)),
        ],
        out_specs=pl.BlockSpec((tm, d), lambda i: (i, 0)),
        out_shape=jax.ShapeDtypeStruct((n, d), BF16),
        compiler_params=_cparams(("arbitrary",)),
        name="entry_norm",
    )(x2, mod3, g)


ROPE_TM = 512


def _rope_kernel(q_ref, k_ref, c_ref, sa_ref, sb_ref, qo_ref, ko_ref):
    c = c_ref[...]
    sa = sa_ref[...]
    sb = sb_ref[...]

    def rope(t):
        return (t * c + pltpu.roll(t, LANES - ROPE_HALF, 1) * sa
                + pltpu.roll(t, ROPE_HALF, 1) * sb)

    ko_ref[...] = rope(k_ref[...].astype(F32)).astype(BF16)
    for m in range(Q_WIDTH // LANES):
        q = q_ref[:, m * LANES:(m + 1) * LANES].astype(F32)
        qo_ref[:, m * LANES:(m + 1) * LANES] = (rope(q) * (HEAD_DIM ** -0.5)).astype(BF16)


def _rope(u, tabs):
    n = u.shape[0]
    tm = _tile(n, ROPE_TM)
    tab_spec = pl.BlockSpec((tm, LANES), lambda i: (i, 0))
    return pl.pallas_call(
        _rope_kernel,
        grid=(n // tm,),
        in_specs=[
            pl.BlockSpec((tm, Q_WIDTH), lambda i: (i, F_WIDTH // Q_WIDTH)),
            pl.BlockSpec((tm, KV_WIDTH), lambda i: (i, COL_K)),
            tab_spec, tab_spec, tab_spec,
        ],
        out_specs=[
            pl.BlockSpec((tm, Q_WIDTH), lambda i: (i, 0)),
            pl.BlockSpec((tm, KV_WIDTH), lambda i: (i, 0)),
        ],
        out_shape=[jax.ShapeDtypeStruct((n, Q_WIDTH), BF16),
                   jax.ShapeDtypeStruct((n, KV_WIDTH), BF16)],
        compiler_params=_cparams(("arbitrary",)),
        name="rope",
    )(u, u, *tabs)


def _rope_tables(positions):
    inv = ROPE_THETA ** (-jnp.arange(0, ROPE_DIM, 2, dtype=F32) / ROPE_DIM)
    ang = positions.astype(F32)[..., None] * inv
    cos = jnp.cos(ang).reshape(-1, ROPE_HALF)
    sin = jnp.sin(ang).reshape(-1, ROPE_HALF)
    n = cos.shape[0]
    rest = HEAD_DIM - ROPE_DIM
    ones = jnp.ones((n, rest), F32)
    z_half = jnp.zeros((n, ROPE_HALF), F32)
    z_rest = jnp.zeros((n, rest), F32)
    reps = LANES // HEAD_DIM
    c = jnp.tile(jnp.concatenate([cos, cos, ones], axis=1), (1, reps))
    sa = jnp.tile(jnp.concatenate([-sin, z_half, z_rest], axis=1), (1, reps))
    sb = jnp.tile(jnp.concatenate([z_half, sin, z_rest], axis=1), (1, reps))
    return c, sa, sb


F_SUB = BF16_ROWS


def _dft_constants(n1_len, n2_len):
    cidx = np.arange(F_GDIM)
    ang = 2.0 * np.pi * np.outer(cidx, cidx) / F_GDIM
    wc = np.concatenate([np.cos(ang), -np.sin(ang)], axis=1) / math.sqrt(F_GDIM)
    a = np.arange(n1_len)
    ang1 = 2.0 * np.pi * np.outer(a, a) / n1_len
    fr, fi = np.cos(ang1), -np.sin(ang1)
    fbig = np.block([[fr, -fi], [fi, fr]]) / math.sqrt(n1_len)
    k1 = np.arange(n1_len)[:, None, None]
    k2 = np.arange(n2_len)[None, :, None]
    b = np.arange(n2_len)[None, None, :]
    ang2 = 2.0 * np.pi * (b * k2 / n2_len + b * k1 / (n1_len * n2_len))
    g = np.concatenate([np.cos(ang2), np.sin(ang2)], axis=2) / math.sqrt(n2_len)
    return (jnp.asarray(wc, BF16), jnp.asarray(fbig, BF16), jnp.asarray(g, BF16))


def _f12_kernel(u_ref, wc_ref, fb_ref, t_ref):
    n1_len = u_ref.shape[1]
    x = u_ref[0].reshape(n1_len * F_SUB, F_WIDTH)
    zr, zi = [], []
    for g in range(F_GROUPS):
        r = jnp.dot(x[:, g * F_GDIM:(g + 1) * F_GDIM], wc_ref[...], preferred_element_type=F32)
        zr.append(r[:, :F_GDIM].astype(BF16))
        zi.append(r[:, F_GDIM:].astype(BF16))
    zr = jnp.concatenate(zr, axis=1).reshape(n1_len, F_SUB, F_WIDTH)
    zi = jnp.concatenate(zi, axis=1).reshape(n1_len, F_SUB, F_WIDTH)
    zr = pltpu.einshape("mnc->nmc", zr)
    zi = pltpu.einshape("mnc->nmc", zi)
    for j in range(F_SUB):
        rhs = jnp.concatenate([zr[j], zi[j]], axis=0)
        t = jnp.dot(fb_ref[...], rhs, preferred_element_type=F32)
        t_ref[0, j] = t.astype(BF16).reshape(2, n1_len, F_WIDTH)


def _f3_kernel(t_ref, g_ref, y_ref):
    tr = pltpu.einshape("nkc->knc", t_ref[0, :, 0])
    ti = pltpu.einshape("nkc->knc", t_ref[0, :, 1])
    ys = []
    for j in range(F_SUB):
        rhs = jnp.concatenate([tr[j], ti[j]], axis=0)
        ys.append(jnp.dot(g_ref[j], rhs, preferred_element_type=F32).astype(BF16))
    y_ref[0] = pltpu.einshape("knc->nkc", jnp.stack(ys, axis=0))


def _fourier(u, n_batch, seq, consts):
    wc, fbig, gmat = consts
    n1_len = fbig.shape[0] // 2
    n2_len = seq // n1_len
    u5 = u.reshape(n_batch, n1_len, n2_len, u.shape[-1])
    t = pl.pallas_call(
        _f12_kernel,
        grid=(n_batch, n2_len // F_SUB),
        in_specs=[
            pl.BlockSpec((1, n1_len, F_SUB, F_WIDTH), lambda b, j: (b, 0, j, 0)),
            pl.BlockSpec(wc.shape, lambda b, j: (0, 0)),
            pl.BlockSpec(fbig.shape, lambda b, j: (0, 0)),
        ],
        out_specs=pl.BlockSpec((1, F_SUB, 2, n1_len, F_WIDTH), lambda b, j: (b, j, 0, 0, 0)),
        out_shape=jax.ShapeDtypeStruct((n_batch, n2_len, 2, n1_len, F_WIDTH), BF16),
        compiler_params=_cparams(("arbitrary", "arbitrary")),
        name="fourier_stage1",
    )(u5, wc, fbig)
    y = pl.pallas_call(
        _f3_kernel,
        grid=(n_batch, n1_len // F_SUB),
        in_specs=[
            pl.BlockSpec((1, n2_len, 2, F_SUB, F_WIDTH), lambda b, j: (b, 0, 0, j, 0)),
            pl.BlockSpec((F_SUB, n2_len, 2 * n2_len), lambda b, j: (j, 0, 0)),
        ],
        out_specs=pl.BlockSpec((1, n2_len, F_SUB, F_WIDTH), lambda b, j: (b, 0, j, 0)),
        out_shape=jax.ShapeDtypeStruct((n_batch, n2_len, n1_len, F_WIDTH), BF16),
        compiler_params=_cparams(("arbitrary", "arbitrary")),
        name="fourier_stage2",
    )(t, gmat)
    return y.reshape(n_batch * seq, F_WIDTH)


def _att_kernel(sink_ref, q_ref, k_ref, v_ref, o_ref, kv_ref, *, seq):
    i = pl.program_id(1)
    s0 = i * ATT_Q
    ws = pl.multiple_of(jnp.clip(s0 - WINDOW, 0, seq - ATT_KW), WINDOW)
    off = s0 - ws
    lane = lax.broadcasted_iota(I32, (ATT_KW, LANES), 1)
    low = lane < HEAD_DIM
    for t, ref in enumerate((k_ref, v_ref)):
        w = ref[0, pl.ds(ws, ATT_KW), :].astype(F32)
        w_sw = pltpu.roll(w, HEAD_DIM, 1)
        for h in range(N_KV):
            own, oth = (w, w_sw) if h == 0 else (w_sw, w)
            kv_ref[t, h, 0] = jnp.where(low, own, 0.0).astype(BF16)
            kv_ref[t, h, 1] = jnp.where(low, 0.0, oth).astype(BF16)

    def sub_block(a, carry):
        r0 = pl.multiple_of(a * ATT_SUB, ATT_SUB)
        c0 = pl.multiple_of(
            jnp.clip(((off + r0) // WINDOW) * WINDOW - WINDOW, 0, ATT_KW - ATT_SW), WINDOW)
        row = lax.broadcasted_iota(I32, (ATT_SUB, ATT_SW), 0)
        col = lax.broadcasted_iota(I32, (ATT_SUB, ATT_SW), 1)
        rel = (c0 + col) - (off + r0 + row)
        bias = jnp.where(jnp.abs(rel) <= WINDOW, 0.0, NEG_INF).astype(F32)
        pairs = [(h, m) for h in range(N_KV) for m in range(HEADS_PER_KV // 2)]
        for g0 in range(0, len(pairs), ATT_GROUP):
            group = [(h, m, half) for (h, m) in pairs[g0:g0 + ATT_GROUP] for half in range(2)]
            scores = []
            for h, m, half in group:
                cb = (h * HEADS_PER_KV // 2 + m) * LANES
                q2 = q_ref[pl.ds(r0, ATT_SUB), cb:cb + LANES]
                kk = kv_ref[0, h, half, pl.ds(c0, ATT_SW), :]
                scores.append(lax.dot_general(q2, kk, (((1,), (1,)), ((), ())),
                                              preferred_element_type=F32))
            probs, rdens = [], []
            for (h, m, half), s in zip(group, scores):
                sink = sink_ref[h * HEADS_PER_KV + 2 * m + half]
                s = s + bias
                mx = jnp.maximum(jnp.max(s, axis=-1, keepdims=True), sink)
                p = jnp.exp(s - mx)
                rdens.append(1.0 / (jnp.sum(p, axis=-1, keepdims=True) + jnp.exp(sink - mx)))
                probs.append(p.astype(BF16))
            outs = []
            for (h, m, half), p, rden in zip(group, probs, rdens):
                vv = kv_ref[1, h, half, pl.ds(c0, ATT_SW), :]
                outs.append(jnp.dot(p, vv, preferred_element_type=F32) * rden)
            for j, (h, m) in enumerate(pairs[g0:g0 + ATT_GROUP]):
                cb = (h * HEADS_PER_KV // 2 + m) * LANES
                o_ref[pl.ds(r0, ATT_SUB), cb:cb + LANES] = (outs[2 * j] + outs[2 * j + 1]).astype(BF16)
        return carry

    lax.fori_loop(0, ATT_Q // ATT_SUB, sub_block, 0)


def _attention(q_rot, k_rot, u, sink, n_batch, seq):
    n = q_rot.shape[0]
    nq = seq // ATT_Q
    k3 = k_rot.reshape(n_batch, seq, KV_WIDTH)
    u3 = u.reshape(n_batch, seq, u.shape[-1])
    return pl.pallas_call(
        functools.partial(_att_kernel, seq=seq),
        grid=(n_batch, nq),
        in_specs=[
            pl.BlockSpec(memory_space=pltpu.SMEM),
            pl.BlockSpec((ATT_Q, Q_WIDTH), lambda b, i: (b * nq + i, 0)),
            pl.BlockSpec((1, seq, KV_WIDTH), lambda b, i: (b, 0, 0)),
            pl.BlockSpec((1, seq, KV_WIDTH), lambda b, i: (b, 0, COL_V)),
        ],
        out_specs=pl.BlockSpec((ATT_Q, Q_WIDTH), lambda b, i: (b * nq + i, 0)),
        out_shape=jax.ShapeDtypeStruct((n, Q_WIDTH), BF16),
        scratch_shapes=[pltpu.VMEM((2, N_KV, 2, ATT_KW, LANES), BF16)],
        compiler_params=_cparams(("arbitrary", "arbitrary")),
        name="window_attention",
    )(sink, q_rot, k3, u3)


D_TM = 512
ROUTER_PAD = LANES


def _d_kernel(f_ref, a_ref, gf_ref, ga_ref, x_ref, mod_ref, wfo_ref, wao_ref, wout_ref, x1_ref):
    yf = jnp.dot(f_ref[...], wfo_ref[...], preferred_element_type=F32)
    ya = jnp.dot(a_ref[...], wao_ref[...], preferred_element_type=F32)
    merged = (jax.nn.sigmoid(gf_ref[...].astype(F32)) * yf
              + jax.nn.sigmoid(ga_ref[...].astype(F32)) * ya)
    out = jnp.dot(merged.astype(BF16), wout_ref[...], preferred_element_type=F32)
    x1_ref[...] = x_ref[...] + mod_ref[0, 2:3, :] * out


def _layer_weight_spec(w, layer):
    return pl.BlockSpec((None,) + w.shape[1:], lambda i: (layer, 0, 0),
                        pipeline_mode=pl.Buffered(1))


def _mix_out(yf, att, u, x2, mod3, wfo, wao, wout, layer, seq):
    n, d = x2.shape
    tm = _tile(seq, D_TM)
    per_batch = seq // tm
    row = lambda w: pl.BlockSpec((tm, w), lambda i: (i, 0))
    return pl.pallas_call(
        _d_kernel,
        grid=(n // tm,),
        in_specs=[
            row(F_WIDTH), row(Q_WIDTH),
            pl.BlockSpec((tm, d), lambda i: (i, COL_GF)),
            pl.BlockSpec((tm, d), lambda i: (i, COL_GA)),
            row(d),
            pl.BlockSpec((1, N_ADA, d), lambda i: (i // per_batch, 0, 0)),
            _layer_weight_spec(wfo, layer), _layer_weight_spec(wao, layer),
            _layer_weight_spec(wout, layer),
        ],
        out_specs=row(d),
        out_shape=jax.ShapeDtypeStruct((n, d), F32),
        compiler_params=_cparams(("arbitrary",)),
        name="mix_out",
    )(yf, att, u, u, x2, mod3, wfo, wao, wout)


R_TN = 1024
R_SUB = 512
U32 = jnp.uint32
HI_MASK = 0xFFFF0000


def _pack_bf16_pairs(hb):
    bits = lax.bitcast_convert_type(hb.astype(F32), U32)
    half = hb.shape[1] // 2
    return (bits[:, half:] & U32(HI_MASK)) | (bits[:, :half] >> U32(16))


def _unpack_bf16_pairs(xu):
    lo = lax.bitcast_convert_type(xu << U32(16), F32).astype(BF16)
    hi = lax.bitcast_convert_type(xu & U32(HI_MASK), F32).astype(BF16)
    return jnp.concatenate([lo, hi], axis=1)


def _route_kernel(bias_ref, x_ref, mod_ref, g2_ref, wr_ref, tri_ref,
                  hp_ref, e_ref, w_ref, p_ref, cnt_ref, carry_ref, hb_ref):
    @pl.when(pl.program_id(0) == 0)
    def _():
        carry_ref[...] = jnp.zeros_like(carry_ref)

    tn = x_ref.shape[0]
    g2 = g2_ref[...]
    shift = mod_ref[0, 3:4, :]
    scale = mod_ref[0, 4:5, :]

    def norm_body(i, c):
        r = pl.multiple_of(i * NORM_ROWS, NORM_ROWS)
        hb = _norm_mod_rows(x_ref[pl.ds(r, NORM_ROWS), :], g2, shift, scale).astype(BF16)
        hb_ref[pl.ds(r, NORM_ROWS), :] = hb
        hp_ref[pl.ds(r, NORM_ROWS), :] = _pack_bf16_pairs(hb)
        return c

    lax.fori_loop(0, tn // NORM_ROWS, norm_body, 0, unroll=NORM_UNROLL)
    logits = jnp.dot(hb_ref[...], wr_ref[...], preferred_element_type=F32)
    lt = logits.T
    sc = [jax.nn.sigmoid(lt[e:e + 1, :]) for e in range(N_EXPERTS)]
    bz = [sc[e] + bias_ref[e] for e in range(N_EXPERTS)]
    one = jnp.ones((1, tn), F32)
    zero = jnp.zeros((1, tn), F32)
    in_top, gscore = [], []
    for g in range(N_GROUPS):
        v = bz[g * EPG:(g + 1) * EPG]
        rank = [zero] * EPG
        for a in range(EPG):
            for b in range(a + 1, EPG):
                a_wins = jnp.where(v[a] >= v[b], one, zero)
                rank[b] = rank[b] + a_wins
                rank[a] = rank[a] + (one - a_wins)
        top = [jnp.where(rank[a] < TOP_K - 0.5, one, zero) for a in range(EPG)]
        in_top += top
        gs = zero
        for a in range(EPG):
            gs = gs + top[a] * v[a]
        gscore.append(gs)
    sel = []
    for g in range(N_GROUPS):
        ok = one
        for other in range(N_GROUPS):
            if other < g:
                ok = ok * jnp.where(gscore[g] > gscore[other], one, zero)
            elif other > g:
                ok = ok * jnp.where(gscore[g] >= gscore[other], one, zero)
        sel += [in_top[g * EPG + a] * ok for a in range(EPG)]

    selm = jnp.concatenate(sel, axis=0)
    carry = carry_ref[:, 0:1]
    cums = []
    for c in range(tn // R_SUB):
        blk = selm[:, c * R_SUB:(c + 1) * R_SUB]
        cums.append(jnp.dot(blk.astype(BF16), tri_ref[...], preferred_element_type=F32) + carry)
        carry = carry + jnp.sum(blk, axis=1, keepdims=True)
    cum = jnp.concatenate(cums, axis=1)
    carry_ref[...] = jnp.broadcast_to(carry, carry_ref.shape)
    cnt_ref[...] = jnp.broadcast_to(carry, cnt_ref.shape)

    before = zero
    eidx = [zero, zero]
    score = [zero, zero]
    pos = [zero, zero]
    for e in range(N_EXPERTS):
        pick = [sel[e] * jnp.where(before < 0.5, one, zero),
                sel[e] * jnp.where(before >= 0.5, one, zero)]
        for k in range(TOP_K):
            eidx[k] = eidx[k] + pick[k] * float(e)
            score[k] = score[k] + pick[k] * sc[e]
            pos[k] = pos[k] + pick[k] * cum[e:e + 1, :]
        before = before + sel[e]
    tot = score[0] + score[1]
    for k in range(TOP_K):
        e_ref[k:k + 1, :] = eidx[k].astype(I32)
        w_ref[k:k + 1, :] = score[k] / tot
        p_ref[k:k + 1, :] = pos[k].astype(I32)


def _route(x1, mod3, g2, wr, router_b, seq):
    n, d = x1.shape
    tn = _tile(seq, R_TN)
    per_batch = seq // tn
    tri = jnp.asarray(np.triu(np.ones((R_SUB, R_SUB), np.float32), k=1), BF16)
    kn = lambda: pl.BlockSpec((TOP_K, tn), lambda i: (0, i))
    return pl.pallas_call(
        _route_kernel,
        grid=(n // tn,),
        in_specs=[
            pl.BlockSpec(memory_space=pltpu.SMEM),
            pl.BlockSpec((tn, d), lambda i: (i, 0)),
            pl.BlockSpec((1, N_ADA, d), lambda i: (i // per_batch, 0, 0)),
            pl.BlockSpec((1, d), lambda i: (0, 0)),
            pl.BlockSpec(wr.shape, lambda i: (0, 0)),
            pl.BlockSpec((R_SUB, R_SUB), lambda i: (0, 0)),
        ],
        out_specs=[pl.BlockSpec((tn, d // 2), lambda i: (i, 0)), kn(), kn(), kn(),
                   pl.BlockSpec((N_EXPERTS, LANES), lambda i: (0, 0))],
        out_shape=[jax.ShapeDtypeStruct((n, d // 2), U32),
                   jax.ShapeDtypeStruct((TOP_K, n), I32), jax.ShapeDtypeStruct((TOP_K, n), F32),
                   jax.ShapeDtypeStruct((TOP_K, n), I32),
                   jax.ShapeDtypeStruct((N_EXPERTS, LANES), F32)],
        scratch_shapes=[pltpu.VMEM((N_EXPERTS, LANES), F32), pltpu.VMEM((tn, d), BF16)],
        compiler_params=_cparams(("arbitrary",)),
        name="route",
    )(router_b, x1, mod3, g2, wr, tri)


P_TM = 1024
P_ZROWS = 256
ISSUE_UNROLL = 8


def _row_copy(src, src_row, dst, dst_row, sem):
    return pltpu.make_async_copy(src.at[pl.ds(src_row, 1), :], dst.at[pl.ds(dst_row, 1), :], sem)


def _dispatch_kernel(padlo_ref, padhi_ref, slot_ref, h_ref, xs_ref, zrow_ref, sem, zsem):
    tm = h_ref.shape[0]

    def issue(i, carry):
        for j in range(ISSUE_UNROLL):
            r = i * ISSUE_UNROLL + j
            for k in range(TOP_K):
                _row_copy(h_ref, r, xs_ref, slot_ref[k, r], sem).start(priority=k)
        return carry

    lax.fori_loop(0, tm // ISSUE_UNROLL, issue, 0)

    @pl.when(pl.program_id(0) == pl.num_programs(0) - 1)
    def _():
        zrow_ref[...] = jnp.zeros_like(zrow_ref)
        zrows = zrow_ref.shape[0]

        def zissue(r, carry):
            _row_copy(zrow_ref, 0, xs_ref, r, zsem).start()
            return carry

        def zwait(r, carry):
            _row_copy(zrow_ref, 0, xs_ref, r, zsem).wait()
            return carry

        for e in range(N_EXPERTS):
            lax.fori_loop(padlo_ref[e], padhi_ref[e], zissue, 0)
            lax.fori_loop(padlo_ref[e], padhi_ref[e], zwait, 0)

        def chunk(j):
            start = pl.multiple_of(padhi_ref[N_EXPERTS - 1] + j * zrows, zrows)
            return pltpu.make_async_copy(zrow_ref, xs_ref.at[pl.ds(start, zrows), :], zsem)

        n_chunks = (xs_ref.shape[0] - padhi_ref[N_EXPERTS - 1]) // zrows
        lax.fori_loop(0, n_chunks, lambda j, c: (chunk(j).start(), c)[1], 0)
        lax.fori_loop(0, n_chunks, lambda j, c: (chunk(j).wait(), c)[1], 0)

    for k in range(TOP_K):
        pltpu.make_async_copy(h_ref, xs_ref.at[pl.ds(0, tm), :], sem).wait()


def _slot_spec(tm):
    return pl.BlockSpec((TOP_K, tm), lambda i, *_: (0, i), memory_space=pltpu.SMEM)


def _dispatch(hp, slot, pad_lo, pad_hi, cap):
    n, dp = hp.shape
    tm = _tile(n, P_TM)
    grid_spec = pltpu.PrefetchScalarGridSpec(
        num_scalar_prefetch=2,
        grid=(n // tm,),
        in_specs=[_slot_spec(tm), pl.BlockSpec((tm, dp), lambda i, *_: (i, 0))],
        out_specs=pl.BlockSpec(memory_space=pl.ANY),
        scratch_shapes=[pltpu.VMEM((P_ZROWS, dp), hp.dtype), pltpu.SemaphoreType.DMA(()),
                        pltpu.SemaphoreType.DMA(())],
    )
    return pl.pallas_call(
        _dispatch_kernel,
        grid_spec=grid_spec,
        out_shape=jax.ShapeDtypeStruct((cap, dp), hp.dtype),
        compiler_params=_cparams(("arbitrary",)),
        name="dispatch",
    )(pad_lo, pad_hi, slot, hp)


def _expert_kernel(be_ref, nu_ref, xs_ref, w1_ref, w3_ref, w2_ref, ys_ref):
    @pl.when(pl.program_id(0) < nu_ref[0])
    def _():
        x = _unpack_bf16_pairs(xs_ref[...])
        a = jnp.dot(x, w1_ref[0], preferred_element_type=F32)
        b = jnp.dot(x, w3_ref[0], preferred_element_type=F32)
        h = (a * jax.nn.sigmoid(a)) * b
        ys_ref[...] = jnp.dot(h.astype(BF16), w2_ref[0], preferred_element_type=F32)

    @pl.when(pl.program_id(0) >= nu_ref[0])
    def _():
        ys_ref[...] = jnp.zeros_like(ys_ref)


def _experts(xs, block_e, n_used, w1, w3, w2, layer):
    cap, dp = xs.shape
    n_blocks = cap // MOE_BLOCK
    d, de = w1.shape[2:]
    blk = lambda i, be, nu: (jnp.minimum(i, nu[0] - 1), 0)
    wsel = lambda i, be, nu: (layer, be[jnp.minimum(i, nu[0] - 1)], 0, 0)
    grid_spec = pltpu.PrefetchScalarGridSpec(
        num_scalar_prefetch=2,
        grid=(n_blocks,),
        in_specs=[
            pl.BlockSpec((MOE_BLOCK, dp), blk),
            pl.BlockSpec((None, 1, d, de), wsel),
            pl.BlockSpec((None, 1, d, de), wsel),
            pl.BlockSpec((None, 1, de, d), wsel),
        ],
        out_specs=pl.BlockSpec((MOE_BLOCK, d), lambda i, be, nu: (i, 0)),
    )
    return pl.pallas_call(
        _expert_kernel,
        grid_spec=grid_spec,
        out_shape=jax.ShapeDtypeStruct((cap, d), F32),
        compiler_params=_cparams(("arbitrary",)),
        name="experts",
    )(block_e, n_used, xs, w1, w3, w2)


C_TM = 512


def _combine_kernel(slot_ref, slotn_ref, x_ref, wb_ref, mod_ref, modn_ref, gn_ref, ys_ref,
                    *rest, final):
    if final:
        o_ref, buf_ref, sem = rest
    else:
        o_ref, hn_ref, buf_ref, sem = rest
    tm = x_ref.shape[0]
    step = pl.program_id(0)
    last = pl.num_programs(0) - 1
    cur = step % 2
    nxt = 1 - cur

    def gather(sl_ref, b, r, k):
        return _row_copy(ys_ref, sl_ref[k, r], buf_ref.at[b, k], r, sem.at[b])

    def wait_buffer(b):
        for k in range(TOP_K):
            pltpu.make_async_copy(ys_ref.at[pl.ds(0, tm), :], buf_ref.at[b, k], sem.at[b]).wait()

    @pl.when(step == 0)
    def _():
        def first(i, carry):
            for j in range(ISSUE_UNROLL):
                for k in range(TOP_K):
                    gather(slot_ref, 0, i * ISSUE_UNROLL + j, k).start(priority=k)
            return carry

        lax.fori_loop(0, tm // ISSUE_UNROLL, first, 0)

    def ahead(i, carry):
        for j in range(ISSUE_UNROLL):
            for k in range(TOP_K):
                gather(slotn_ref, nxt, i * ISSUE_UNROLL + j, k).start(priority=k)
        return carry

    lax.fori_loop(0, tm // ISSUE_UNROLL, ahead, 0)
    wait_buffer(cur)

    reps = x_ref.shape[1] // LANES
    g2 = mod_ref[0, 5:6, :]
    gn = gn_ref[...]
    shift = modn_ref[0, 0:1, :]
    scale = modn_ref[0, 1:2, :]

    def rows_body(i, carry):
        rows = pl.ds(pl.multiple_of(i * NORM_ROWS, NORM_ROWS), NORM_ROWS)
        y = (jnp.tile(wb_ref[0, rows, :], (1, reps)) * buf_ref[cur, 0, rows, :]
             + jnp.tile(wb_ref[1, rows, :], (1, reps)) * buf_ref[cur, 1, rows, :])
        out = x_ref[rows, :] + g2 * y
        if final:
            ms = jnp.mean(out * out, axis=-1, keepdims=True)
            o_ref[rows, :] = out * lax.rsqrt(ms + EPS) * gn
        else:
            o_ref[rows, :] = out
            hn_ref[rows, :] = _norm_mod_rows(out, gn, shift, scale).astype(hn_ref.dtype)
        return carry

    lax.fori_loop(0, tm // NORM_ROWS, rows_body, 0, unroll=NORM_UNROLL)

    @pl.when(step == last)
    def _():
        wait_buffer(nxt)


def _combine(x1, ys, slot, wb, mod3, modn3, gn, seq, final):
    n, d = x1.shape
    tm = _tile(seq, C_TM)
    per_batch = seq // tm
    row = pl.BlockSpec((tm, d), lambda i, *_: (i, 0))
    mod_spec = pl.BlockSpec((1, N_ADA, d), lambda i, *_: (i // per_batch, 0, 0))
    grid_spec = pltpu.PrefetchScalarGridSpec(
        num_scalar_prefetch=0,
        grid=(n // tm,),
        in_specs=[
            _slot_spec(tm),
            pl.BlockSpec((TOP_K, tm), lambda i, *_: (0, jnp.minimum(i + 1, n // tm - 1)),
                         memory_space=pltpu.SMEM),
            row,
            pl.BlockSpec((TOP_K, tm, LANES), lambda i, *_: (0, i, 0)),
            mod_spec, mod_spec,
            pl.BlockSpec((1, d), lambda i, *_: (0, 0)),
            pl.BlockSpec(memory_space=pl.ANY),
        ],
        out_specs=row if final else [row, row],
        scratch_shapes=[pltpu.VMEM((2, TOP_K, tm, d), F32), pltpu.SemaphoreType.DMA((2,))],
    )
    x_sds = jax.ShapeDtypeStruct((n, d), F32)
    return pl.pallas_call(
        functools.partial(_combine_kernel, final=final),
        grid_spec=grid_spec,
        out_shape=x_sds if final else [x_sds, jax.ShapeDtypeStruct((n, d), BF16)],
        compiler_params=_cparams(("arbitrary",)),
        name="combine",
    )(slot, slot, x1, wb, mod3, modn3, gn, ys)


def _moe(x1, mod3, g2, modn3, gn, wr, router_b, w1, w3, w2, layer, seq, final):
    n, d = x1.shape
    hp, eidx, gw, pos, cnt = _route(x1, mod3, g2, wr, router_b, seq)
    counts = cnt[:, 0].astype(I32)
    padded = (counts + MOE_BLOCK - 1) // MOE_BLOCK * MOE_BLOCK
    pends = jnp.cumsum(padded)
    pstarts = (pends - padded).astype(I32)
    n_blocks = -(-(n * TOP_K) // MOE_BLOCK) + N_EXPERTS
    block_lo = jnp.arange(n_blocks, dtype=I32) * MOE_BLOCK
    block_e = jnp.minimum(jnp.sum((pends[None, :] <= block_lo[:, None]).astype(I32), axis=1),
                          N_EXPERTS - 1).astype(I32)
    n_used = (pends[-1:] // MOE_BLOCK).astype(I32)
    seg = jnp.sum(jnp.where(eidx[:, :, None] == jnp.arange(N_EXPERTS, dtype=I32), pstarts, 0),
                  axis=-1)
    slot = (seg + pos).astype(I32)
    xs = _dispatch(hp, slot, (pstarts + counts).astype(I32), pends.astype(I32),
                   n_blocks * MOE_BLOCK)
    ys = _experts(xs, block_e, n_used, w1, w3, w2, layer)
    wb = jnp.broadcast_to(gw[:, :, None], (TOP_K, n, LANES))
    return _combine(x1, ys, slot, wb, mod3, modn3, gn, seq, final)


def _fft_split(seq):
    n1_len = 1 << (int(math.log2(seq)) // 2)
    return n1_len, seq // n1_len


def kernel(x, c, positions, ada_w, ada_b, norm1_g, norm2_g, w_in, w_fourier_out, w_attn_out,
           attn_sink, w_out, router_w, router_b, expert_w1, expert_w3, expert_w2, final_norm_g):
    n_batch, seq, d = x.shape
    depth = ada_w.shape[0]
    n1_len, n2_len = _fft_split(seq)
    assert n1_len * n2_len == seq and n1_len % F_SUB == 0 and n2_len % F_SUB == 0
    assert seq % ATT_Q == 0 and seq >= ATT_KW and d == D_MODEL

    mod = _ada_mod(c, ada_w, ada_b).reshape(depth, n_batch, N_ADA, d)
    tabs = _rope_tables(positions)
    consts = _dft_constants(n1_len, n2_len)
    w_in_bf = _reorder_cast_w_in(w_in)
    wr = jnp.pad(router_w, ((0, 0), (0, ROUTER_PAD - N_EXPERTS))).astype(BF16)
    wfo, wao, wout = (w.astype(BF16) for w in (w_fourier_out, w_attn_out, w_out))
    ew1, ew3, ew2 = (w.astype(BF16) for w in (expert_w1, expert_w3, expert_w2))

    x2 = x.reshape(n_batch * seq, d)
    h = _entry_norm(x2, mod[0], norm1_g[0].reshape(1, d), seq)
    for l in range(depth):
        final = l == depth - 1
        u = _in_proj(h, w_in_bf, l)
        yf = _fourier(u, n_batch, seq, consts)
        q_rot, k_rot = _rope(u, tabs)
        att = _attention(q_rot, k_rot, u, attn_sink[l], n_batch, seq)
        x1 = _mix_out(yf, att, u, x2, mod[l], wfo, wao, wout, l, seq)
        gn = (final_norm_g if final else norm1_g[l + 1]).reshape(1, d)
        res = _moe(x1, mod[l], norm2_g[l].reshape(1, d), mod[l if final else l + 1], gn, wr,
                   router_b, ew1, ew3, ew2, l, seq, final)
        if final:
            x2 = res
        else:
            x2, h = res
    return x2.reshape(n_batch, seq, d)
```

```python
import functools
import math

import numpy as np
import jax
import jax.numpy as jnp
from jax import lax
from jax.experimental import pallas as pl
from jax.experimental.pallas import tpu as pltpu

F32 = jnp.float32
BF16 = jnp.bfloat16
I32 = jnp.int32
U32 = jnp.uint32

D_MODEL = 2048
N_ADA = 6
F_GROUPS = 4
F_GDIM = 256
F_WIDTH = F_GROUPS * F_GDIM
N_HEADS = 16
N_KV = 2
HEAD_DIM = 64
HEADS_PER_KV = N_HEADS // N_KV
Q_WIDTH = N_HEADS * HEAD_DIM
KV_WIDTH = N_KV * HEAD_DIM
ROPE_DIM = HEAD_DIM // 4
ROPE_HALF = ROPE_DIM // 2
ROPE_THETA = 500000.0
WINDOW = 128
N_EXPERTS = 16
N_GROUPS = 4
EPG = N_EXPERTS // N_GROUPS
TOP_K = 2
D_EXPERT = 768
MOE_BLOCK = 512
EPS = 1e-6
NEG_INF = -1e30
IN_WIDTH = F_WIDTH + Q_WIDTH + 2 * KV_WIDTH + 2 * D_MODEL

COL_GF = (F_WIDTH + Q_WIDTH) // D_MODEL
COL_GA = COL_GF + 1
COL_K = (F_WIDTH + Q_WIDTH + 2 * D_MODEL) // KV_WIDTH
COL_V = COL_K + 1

LANES = 128
SUBLANES = 8
BF16_ROWS = 16
VMEM_LIMIT = 56 * 1024 * 1024

ATT_Q = 256
ATT_KW = ATT_Q + 2 * WINDOW
ATT_SUB = 128
ATT_SW = ATT_SUB + 2 * WINDOW
ATT_GROUP = 4


def _cparams(sem, vmem=VMEM_LIMIT):
    return pltpu.CompilerParams(dimension_semantics=sem, vmem_limit_bytes=vmem)


def _tile(n, pref):
    t = min(n, pref)
    while n % t:
        t //= 2
    return t


ADA_TN = 768
ADA_ROWS = 32


def _ada_kernel(cb_ref, w_ref, b_ref, o_ref, *, n_batch):
    tn = w_ref.shape[-1]
    n_chunks = w_ref.shape[1] // ADA_ROWS

    def body(i, accs):
        r = pl.multiple_of(i * ADA_ROWS, ADA_ROWS)
        w = w_ref[0, pl.ds(r, ADA_ROWS), :].reshape(ADA_ROWS // SUBLANES, SUBLANES, tn)
        out = []
        for b in range(n_batch):
            c = cb_ref[b, pl.ds(r, ADA_ROWS), :]
            ca = c * jax.nn.sigmoid(c)
            cr = jnp.tile(ca, (1, tn // LANES)).reshape(ADA_ROWS // SUBLANES, SUBLANES, tn)
            out.append(accs[b] + jnp.sum(w * cr, axis=0))
        return tuple(out)

    accs = lax.fori_loop(0, n_chunks, body,
                         tuple(jnp.zeros((SUBLANES, tn), F32) for _ in range(n_batch)))
    for b in range(n_batch):
        o_ref[0, b:b + 1, :] = jnp.sum(accs[b], axis=0, keepdims=True) + b_ref[0]


def _ada_mod(c, ada_w, ada_b):
    n_layers, d, width = ada_w.shape
    n_batch = c.shape[0]
    cb = jnp.broadcast_to(c[:, :, None], (n_batch, d, LANES))
    return pl.pallas_call(
        functools.partial(_ada_kernel, n_batch=n_batch),
        grid=(n_layers, width // ADA_TN),
        in_specs=[
            pl.BlockSpec((n_batch, d, LANES), lambda l, j: (0, 0, 0)),
            pl.BlockSpec((1, d, ADA_TN), lambda l, j: (l, 0, j)),
            pl.BlockSpec((1, 1, ADA_TN), lambda l, j: (l, 0, j)),
        ],
        out_specs=pl.BlockSpec((1, n_batch, ADA_TN), lambda l, j: (l, 0, j)),
        out_shape=jax.ShapeDtypeStruct((n_layers, n_batch, width), F32),
        compiler_params=_cparams(("arbitrary", "arbitrary")),
        name="ada_mod",
    )(cb, ada_w, ada_b.reshape(n_layers, 1, width))


NORM_ROWS = 16
NORM_UNROLL = 8


def _norm_mod_rows(x, g, shift, scale):
    ms = jnp.mean(x * x, axis=-1, keepdims=True)
    y = x * lax.rsqrt(ms + EPS) * g
    return y * (1.0 + scale) + shift


A_TM = 1024
A_TN = 3200


def _a_kernel(h_ref, w_ref, o_ref):
    o_ref[...] = jnp.dot(h_ref[...], w_ref[...], preferred_element_type=F32).astype(o_ref.dtype)


def _in_proj(h, w_bf, layer):
    n, d = h.shape
    width = w_bf.shape[-1]
    tm = _tile(n, A_TM)
    tn = _tile(width, A_TN)
    return pl.pallas_call(
        _a_kernel,
        grid=(n // tm, width // tn),
        in_specs=[
            pl.BlockSpec((tm, d), lambda i, j: (i, 0)),
            pl.BlockSpec((None, d, tn), lambda i, j: (layer, 0, j)),
        ],
        out_specs=pl.BlockSpec((tm, tn), lambda i, j: (i, j)),
        out_shape=jax.ShapeDtypeStruct((n, width), BF16),
        compiler_params=_cparams(("arbitrary", "arbitrary")),
        name="in_proj",
    )(h, w_bf)


W_TILE = 2 * KV_WIDTH


def _cast_kernel(w_ref, o_ref):
    o_ref[...] = w_ref[...].astype(o_ref.dtype)


def _reorder_cast_w_in(w_in):
    depth, d, width = w_in.shape
    kv_tile = (F_WIDTH + Q_WIDTH) // W_TILE
    n_tiles = width // W_TILE
    assert (F_WIDTH + Q_WIDTH) % W_TILE == 0 and width % W_TILE == 0

    def src_tile(j):
        return jnp.where(j < kv_tile, j, jnp.where(j < n_tiles - 1, j + 1, kv_tile))

    return pl.pallas_call(
        _cast_kernel,
        grid=(depth, n_tiles),
        in_specs=[pl.BlockSpec((1, d, W_TILE), lambda l, j: (l, 0, src_tile(j)))],
        out_specs=pl.BlockSpec((1, d, W_TILE), lambda l, j: (l, 0, j)),
        out_shape=jax.ShapeDtypeStruct((depth, d, width), BF16),
        compiler_params=_cparams(("arbitrary", "arbitrary")),
        name="w_in_cast",
    )(w_in)


N_TM = 512


def _norm_rows_to(x_ref, dst_ref, g, shift, scale):
    def body(i, carry):
        r = pl.multiple_of(i * NORM_ROWS, NORM_ROWS)
        h = _norm_mod_rows(x_ref[pl.ds(r, NORM_ROWS), :], g, shift, scale)
        dst_ref[pl.ds(r, NORM_ROWS), :] = h.astype(dst_ref.dtype)
        return carry

    lax.fori_loop(0, x_ref.shape[0] // NORM_ROWS, body, 0, unroll=NORM_UNROLL)


def _norm_kernel(x_ref, mod_ref, g_ref, h_ref):
    _norm_rows_to(x_ref, h_ref, g_ref[...], mod_ref[0, 0:1, :], mod_ref[0, 1:2, :])


def _entry_norm(x2, mod3, g, seq):
    n, d = x2.shape
    tm = _tile(seq, N_TM)
    per_batch = seq // tm
    return pl.pallas_call(
        _norm_kernel,
        grid=(n // tm,),
        in_specs=[
            pl.BlockSpec((tm, d), lambda i: (i, 0)),
            pl.BlockSpec((1, N_ADA, d), lambda i: (i // per_batch, 0, 0)),
            pl.BlockSpec((1, d), lambda i: (0, 0)),
        ],
        out_specs=pl.BlockSpec((tm, d), lambda i: (i, 0)),
        out_shape=jax.ShapeDtypeStruct((n, d), BF16),
        compiler_params=_cparams(("arbitrary",)),
        name="entry_norm",
    )(x2, mod3, g)


ROPE_TM = 512


def _rope_kernel(q_ref, k_ref, c_ref, sa_ref, sb_ref, qo_ref, ko_ref):
    c = c_ref[...]
    sa = sa_ref[...]
    sb = sb_ref[...]

    def rope(t):
        return (t * c + pltpu.roll(t, LANES - ROPE_HALF, 1) * sa
                + pltpu.roll(t, ROPE_HALF, 1) * sb)

    ko_ref[...] = rope(k_ref[...].astype(F32)).astype(BF16)
    for m in range(Q_WIDTH // LANES):
        q = q_ref[:, m * LANES:(m + 1) * LANES].astype(F32)
        qo_ref[:, m * LANES:(m + 1) * LANES] = (rope(q) * (HEAD_DIM ** -0.5)).astype(BF16)


def _rope(u, tabs):
    n = u.shape[0]
    tm = _tile(n, ROPE_TM)
    tab_spec = pl.BlockSpec((tm, LANES), lambda i: (i, 0))
    return pl.pallas_call(
        _rope_kernel,
        grid=(n // tm,),
        in_specs=[
            pl.BlockSpec((tm, Q_WIDTH), lambda i: (i, F_WIDTH // Q_WIDTH)),
            pl.BlockSpec((tm, KV_WIDTH), lambda i: (i, COL_K)),
            tab_spec, tab_spec, tab_spec,
        ],
        out_specs=[
            pl.BlockSpec((tm, Q_WIDTH), lambda i: (i, 0)),
            pl.BlockSpec((tm, KV_WIDTH), lambda i: (i, 0)),
        ],
        out_shape=[jax.ShapeDtypeStruct((n, Q_WIDTH), BF16),
                   jax.ShapeDtypeStruct((n, KV_WIDTH), BF16)],
        compiler_params=_cparams(("arbitrary",)),
        name="rope",
    )(u, u, *tabs)


def _rope_tables(positions):
    inv = ROPE_THETA ** (-jnp.arange(0, ROPE_DIM, 2, dtype=F32) / ROPE_DIM)
    ang = positions.astype(F32)[..., None] * inv
    cos = jnp.cos(ang).reshape(-1, ROPE_HALF)
    sin = jnp.sin(ang).reshape(-1, ROPE_HALF)
    n = cos.shape[0]
    rest = HEAD_DIM - ROPE_DIM
    ones = jnp.ones((n, rest), F32)
    z_half = jnp.zeros((n, ROPE_HALF), F32)
    z_rest = jnp.zeros((n, rest), F32)
    reps = LANES // HEAD_DIM
    c = jnp.tile(jnp.concatenate([cos, cos, ones], axis=1), (1, reps))
    sa = jnp.tile(jnp.concatenate([-sin, z_half, z_rest], axis=1), (1, reps))
    sb = jnp.tile(jnp.concatenate([z_half, sin, z_rest], axis=1), (1, reps))
    return c, sa, sb


F_SUB = BF16_ROWS


def _dft_constants(n1_len, n2_len):
    cidx = np.arange(F_GDIM)
    ang = 2.0 * np.pi * np.outer(cidx, cidx) / F_GDIM
    wc = np.concatenate([np.cos(ang), -np.sin(ang)], axis=1) / math.sqrt(F_GDIM)
    a = np.arange(n1_len)
    ang1 = 2.0 * np.pi * np.outer(a, a) / n1_len
    fr, fi = np.cos(ang1), -np.sin(ang1)
    fbig = np.block([[fr, -fi], [fi, fr]]) / math.sqrt(n1_len)
    k1 = np.arange(n1_len)[:, None, None]
    k2 = np.arange(n2_len)[None, :, None]
    b = np.arange(n2_len)[None, None, :]
    ang2 = 2.0 * np.pi * (b * k2 / n2_len + b * k1 / (n1_len * n2_len))
    g = np.concatenate([np.cos(ang2), np.sin(ang2)], axis=2) / math.sqrt(n2_len)
    return (jnp.asarray(wc, BF16), jnp.asarray(fbig, BF16), jnp.asarray(g, BF16))


def _f12_kernel(u_ref, wc_ref, fb_ref, t_ref):
    n1_len = u_ref.shape[1]
    x = u_ref[0].reshape(n1_len * F_SUB, F_WIDTH)
    zr, zi = [], []
    for g in range(F_GROUPS):
        r = jnp.dot(x[:, g * F_GDIM:(g + 1) * F_GDIM], wc_ref[...], preferred_element_type=F32)
        zr.append(r[:, :F_GDIM].astype(BF16))
        zi.append(r[:, F_GDIM:].astype(BF16))
    zr = jnp.concatenate(zr, axis=1).reshape(n1_len, F_SUB, F_WIDTH)
    zi = jnp.concatenate(zi, axis=1).reshape(n1_len, F_SUB, F_WIDTH)
    zr = pltpu.einshape("mnc->nmc", zr)
    zi = pltpu.einshape("mnc->nmc", zi)
    for j in range(F_SUB):
        rhs = jnp.concatenate([zr[j], zi[j]], axis=0)
        t = jnp.dot(fb_ref[...], rhs, preferred_element_type=F32)
        t_ref[0, j] = t.astype(BF16).reshape(2, n1_len, F_WIDTH)


def _f3_kernel(t_ref, g_ref, y_ref):
    tr = pltpu.einshape("nkc->knc", t_ref[0, :, 0])
    ti = pltpu.einshape("nkc->knc", t_ref[0, :, 1])
    ys = []
    for j in range(F_SUB):
        rhs = jnp.concatenate([tr[j], ti[j]], axis=0)
        ys.append(jnp.dot(g_ref[j], rhs, preferred_element_type=F32).astype(BF16))
    y_ref[0] = pltpu.einshape("knc->nkc", jnp.stack(ys, axis=0))


def _fourier(u, n_batch, seq, consts):
    wc, fbig, gmat = consts
    n1_len = fbig.shape[0] // 2
    n2_len = seq // n1_len
    u5 = u.reshape(n_batch, n1_len, n2_len, u.shape[-1])
    t = pl.pallas_call(
        _f12_kernel,
        grid=(n_batch, n2_len // F_SUB),
        in_specs=[
            pl.BlockSpec((1, n1_len, F_SUB, F_WIDTH), lambda b, j: (b, 0, j, 0)),
            pl.BlockSpec(wc.shape, lambda b, j: (0, 0)),
            pl.BlockSpec(fbig.shape, lambda b, j: (0, 0)),
        ],
        out_specs=pl.BlockSpec((1, F_SUB, 2, n1_len, F_WIDTH), lambda b, j: (b, j, 0, 0, 0)),
        out_shape=jax.ShapeDtypeStruct((n_batch, n2_len, 2, n1_len, F_WIDTH), BF16),
        compiler_params=_cparams(("arbitrary", "arbitrary")),
        name="fourier_stage1",
    )(u5, wc, fbig)
    y = pl.pallas_call(
        _f3_kernel,
        grid=(n_batch, n1_len // F_SUB),
        in_specs=[
            pl.BlockSpec((1, n2_len, 2, F_SUB, F_WIDTH), lambda b, j: (b, 0, 0, j, 0)),
            pl.BlockSpec((F_SUB, n2_len, 2 * n2_len), lambda b, j: (j, 0, 0)),
        ],
        out_specs=pl.BlockSpec((1, n2_len, F_SUB, F_WIDTH), lambda b, j: (b, 0, j, 0)),
        out_shape=jax.ShapeDtypeStruct((n_batch, n2_len, n1_len, F_WIDTH), BF16),
        compiler_params=_cparams(("arbitrary", "arbitrary")),
        name="fourier_stage2",
    )(t, gmat)
    return y.reshape(n_batch * seq, F_WIDTH)


def _att_kernel(sink_ref, q_ref, k_ref, v_ref, o_ref, kv_ref, *, seq):
    i = pl.program_id(1)
    s0 = i * ATT_Q
    ws = pl.multiple_of(jnp.clip(s0 - WINDOW, 0, seq - ATT_KW), WINDOW)
    off = s0 - ws
    lane = lax.broadcasted_iota(I32, (ATT_KW, LANES), 1)
    low = lane < HEAD_DIM
    for t, ref in enumerate((k_ref, v_ref)):
        w = ref[0, pl.ds(ws, ATT_KW), :].astype(F32)
        w_sw = pltpu.roll(w, HEAD_DIM, 1)
        for h in range(N_KV):
            own, oth = (w, w_sw) if h == 0 else (w_sw, w)
            kv_ref[t, h, 0] = jnp.where(low, own, 0.0).astype(BF16)
            kv_ref[t, h, 1] = jnp.where(low, 0.0, oth).astype(BF16)

    def sub_block(a, carry):
        r0 = pl.multiple_of(a * ATT_SUB, ATT_SUB)
        c0 = pl.multiple_of(
            jnp.clip(((off + r0) // WINDOW) * WINDOW - WINDOW, 0, ATT_KW - ATT_SW), WINDOW)
        row = lax.broadcasted_iota(I32, (ATT_SUB, ATT_SW), 0)
        col = lax.broadcasted_iota(I32, (ATT_SUB, ATT_SW), 1)
        rel = (c0 + col) - (off + r0 + row)
        bias = jnp.where(jnp.abs(rel) <= WINDOW, 0.0, NEG_INF).astype(F32)
        pairs = [(h, m) for h in range(N_KV) for m in range(HEADS_PER_KV // 2)]
        for g0 in range(0, len(pairs), ATT_GROUP):
            group = [(h, m, half) for (h, m) in pairs[g0:g0 + ATT_GROUP] for half in range(2)]
            scores = []
            for h, m, half in group:
                cb = (h * HEADS_PER_KV // 2 + m) * LANES
                q2 = q_ref[pl.ds(r0, ATT_SUB), cb:cb + LANES]
                kk = kv_ref[0, h, half, pl.ds(c0, ATT_SW), :]
                scores.append(lax.dot_general(q2, kk, (((1,), (1,)), ((), ())),
                                              preferred_element_type=F32))
            probs, rdens = [], []
            for (h, m, half), s in zip(group, scores):
                sink = sink_ref[h * HEADS_PER_KV + 2 * m + half]
                s = s + bias
                mx = jnp.maximum(jnp.max(s, axis=-1, keepdims=True), sink)
                p = jnp.exp(s - mx)
                rdens.append(1.0 / (jnp.sum(p, axis=-1, keepdims=True) + jnp.exp(sink - mx)))
                probs.append(p.astype(BF16))
            outs = []
            for (h, m, half), p, rden in zip(group, probs, rdens):
                vv = kv_ref[1, h, half, pl.ds(c0, ATT_SW), :]
                outs.append(jnp.dot(p, vv, preferred_element_type=F32) * rden)
            for j, (h, m) in enumerate(pairs[g0:g0 + ATT_GROUP]):
                cb = (h * HEADS_PER_KV // 2 + m) * LANES
                o_ref[pl.ds(r0, ATT_SUB), cb:cb + LANES] = (outs[2 * j] + outs[2 * j + 1]).astype(BF16)
        return carry

    lax.fori_loop(0, ATT_Q // ATT_SUB, sub_block, 0)


def _attention(q_rot, k_rot, u, sink, n_batch, seq):
    n = q_rot.shape[0]
    nq = seq // ATT_Q
    k3 = k_rot.reshape(n_batch, seq, KV_WIDTH)
    u3 = u.reshape(n_batch, seq, u.shape[-1])
    return pl.pallas_call(
        functools.partial(_att_kernel, seq=seq),
        grid=(n_batch, nq),
        in_specs=[
            pl.BlockSpec(memory_space=pltpu.SMEM),
            pl.BlockSpec((ATT_Q, Q_WIDTH), lambda b, i: (b * nq + i, 0)),
            pl.BlockSpec((1, seq, KV_WIDTH), lambda b, i: (b, 0, 0)),
            pl.BlockSpec((1, seq, KV_WIDTH), lambda b, i: (b, 0, COL_V)),
        ],
        out_specs=pl.BlockSpec((ATT_Q, Q_WIDTH), lambda b, i: (b * nq + i, 0)),
        out_shape=jax.ShapeDtypeStruct((n, Q_WIDTH), BF16),
        scratch_shapes=[pltpu.VMEM((2, N_KV, 2, ATT_KW, LANES), BF16)],
        compiler_params=_cparams(("arbitrary", "arbitrary")),
        name="window_attention",
    )(sink, q_rot, k3, u3)


D_TM = 512
ROUTER_PAD = LANES


def _d_kernel(f_ref, a_ref, gf_ref, ga_ref, x_ref, mod_ref, wfo_ref, wao_ref, wout_ref, x1_ref):
    yf = jnp.dot(f_ref[...], wfo_ref[...], preferred_element_type=F32)
    ya = jnp.dot(a_ref[...], wao_ref[...], preferred_element_type=F32)
    merged = (jax.nn.sigmoid(gf_ref[...].astype(F32)) * yf
              + jax.nn.sigmoid(ga_ref[...].astype(F32)) * ya)
    out = jnp.dot(merged.astype(BF16), wout_ref[...], preferred_element_type=F32)
    x1_ref[...] = x_ref[...] + mod_ref[0, 2:3, :] * out


def _layer_weight_spec(w, layer):
    return pl.BlockSpec((None,) + w.shape[1:], lambda i: (layer, 0, 0),
                        pipeline_mode=pl.Buffered(1))


def _mix_out(yf, att, u, x2, mod3, wfo, wao, wout, layer, seq):
    n, d = x2.shape
    tm = _tile(seq, D_TM)
    per_batch = seq // tm
    row = lambda w: pl.BlockSpec((tm, w), lambda i: (i, 0))
    return pl.pallas_call(
        _d_kernel,
        grid=(n // tm,),
        in_specs=[
            row(F_WIDTH), row(Q_WIDTH),
            pl.BlockSpec((tm, d), lambda i: (i, COL_GF)),
            pl.BlockSpec((tm, d), lambda i: (i, COL_GA)),
            row(d),
            pl.BlockSpec((1, N_ADA, d), lambda i: (i // per_batch, 0, 0)),
            _layer_weight_spec(wfo, layer), _layer_weight_spec(wao, layer),
            _layer_weight_spec(wout, layer),
        ],
        out_specs=row(d),
        out_shape=jax.ShapeDtypeStruct((n, d), F32),
        compiler_params=_cparams(("arbitrary",)),
        name="mix_out",
    )(yf, att, u, u, x2, mod3, wfo, wao, wout)


R_TN = 1024
R_SUB = 512
PAIRS = [(a, b) for a in range(EPG) for b in range(a + 1, EPG)]
N_SEG = N_GROUPS * len(PAIRS)
SEG_PAD = 32
SEG_A = [g * EPG + a for g in range(N_GROUPS) for a, _ in PAIRS]
SEG_B = [g * EPG + b for g in range(N_GROUPS) for _, b in PAIRS]
HI_MASK = 0xFFFF0000


def _pack_bf16_pairs(hb):
    bits = lax.bitcast_convert_type(hb.astype(F32), U32)
    half = hb.shape[1] // 2
    return (bits[:, half:] & U32(HI_MASK)) | (bits[:, :half] >> U32(16))


def _unpack_pairs_f32(xu):
    return (lax.bitcast_convert_type(xu << U32(16), F32),
            lax.bitcast_convert_type(xu & U32(HI_MASK), F32))


def _route_kernel(bias_ref, x_ref, mod_ref, g2_ref, wr_ref, tri_ref,
                  hp_ref, sp_ref, cnt_ref, carry_ref, hb_ref):
    @pl.when(pl.program_id(0) == 0)
    def _():
        carry_ref[...] = jnp.zeros_like(carry_ref)

    tn, d = x_ref.shape
    half = d // 2
    g2 = g2_ref[...]
    shift = mod_ref[0, 3:4, :]
    scale = mod_ref[0, 4:5, :]

    def norm_body(i, c):
        r = pl.multiple_of(i * NORM_ROWS, NORM_ROWS)
        hb = _norm_mod_rows(x_ref[pl.ds(r, NORM_ROWS), :], g2, shift, scale).astype(BF16)
        hb_ref[pl.ds(r, NORM_ROWS), :] = hb
        hp_ref[pl.ds(r, NORM_ROWS), :half] = _pack_bf16_pairs(hb)
        return c

    lax.fori_loop(0, tn // NORM_ROWS, norm_body, 0, unroll=NORM_UNROLL)
    logits = jnp.dot(hb_ref[...], wr_ref[...], preferred_element_type=F32)
    lt = logits.T
    sc = [jax.nn.sigmoid(lt[e:e + 1, :]) for e in range(N_EXPERTS)]
    bz = [sc[e] + bias_ref[e] for e in range(N_EXPERTS)]
    one = jnp.ones((1, tn), F32)
    zero = jnp.zeros((1, tn), F32)
    in_top, gscore = [], []
    for g in range(N_GROUPS):
        v = bz[g * EPG:(g + 1) * EPG]
        rank = [zero] * EPG
        for a in range(EPG):
            for b in range(a + 1, EPG):
                a_wins = jnp.where(v[a] >= v[b], one, zero)
                rank[b] = rank[b] + a_wins
                rank[a] = rank[a] + (one - a_wins)
        top = [jnp.where(rank[a] < TOP_K - 0.5, one, zero) for a in range(EPG)]
        in_top += top
        gs = zero
        for a in range(EPG):
            gs = gs + top[a] * v[a]
        gscore.append(gs)
    sel = []
    for g in range(N_GROUPS):
        ok = one
        for other in range(N_GROUPS):
            if other < g:
                ok = ok * jnp.where(gscore[g] > gscore[other], one, zero)
            elif other > g:
                ok = ok * jnp.where(gscore[g] >= gscore[other], one, zero)
        sel += [in_top[g * EPG + a] * ok for a in range(EPG)]

    segsel = [sel[g * EPG + a] * sel[g * EPG + b] for g in range(N_GROUPS) for a, b in PAIRS]
    selm = jnp.concatenate(segsel + [zero] * (SEG_PAD - N_SEG), axis=0)
    carry = carry_ref[:, 0:1]
    cums = []
    for c in range(tn // R_SUB):
        blk = selm[:, c * R_SUB:(c + 1) * R_SUB]
        cums.append(jnp.dot(blk.astype(BF16), tri_ref[...], preferred_element_type=F32) + carry)
        carry = carry + jnp.sum(blk, axis=1, keepdims=True)
    cum = jnp.concatenate(cums, axis=1)
    carry_ref[...] = jnp.broadcast_to(carry, carry_ref.shape)
    cnt_ref[...] = jnp.broadcast_to(carry, cnt_ref.shape)

    seg = zero
    pos = zero
    for s_id in range(N_SEG):
        seg = seg + segsel[s_id] * float(s_id)
        pos = pos + segsel[s_id] * cum[s_id:s_id + 1, :]
    sp_ref[0:1, :] = seg.astype(I32)
    sp_ref[1:2, :] = pos.astype(I32)

    before = zero
    score = [zero, zero]
    for e in range(N_EXPERTS):
        score[0] = score[0] + sel[e] * jnp.where(before < 0.5, one, zero) * sc[e]
        score[1] = score[1] + sel[e] * jnp.where(before >= 0.5, one, zero) * sc[e]
        before = before + sel[e]
    tot = score[0] + score[1]
    rows = lax.broadcasted_iota(I32, (LANES, tn), 0)
    wmat = jnp.where(rows == 0, score[0] / tot, jnp.where(rows == 1, score[1] / tot, 0.0))
    hp_ref[:, half:] = lax.bitcast_convert_type(wmat.T, U32)


def _route(x1, mod3, g2, wr, router_b, seq):
    n, d = x1.shape
    tn = _tile(seq, R_TN)
    per_batch = seq // tn
    tri = jnp.asarray(np.triu(np.ones((R_SUB, R_SUB), np.float32), k=1), BF16)
    return pl.pallas_call(
        _route_kernel,
        grid=(n // tn,),
        in_specs=[
            pl.BlockSpec(memory_space=pltpu.SMEM),
            pl.BlockSpec((tn, d), lambda i: (i, 0)),
            pl.BlockSpec((1, N_ADA, d), lambda i: (i // per_batch, 0, 0)),
            pl.BlockSpec((1, d), lambda i: (0, 0)),
            pl.BlockSpec(wr.shape, lambda i: (0, 0)),
            pl.BlockSpec((R_SUB, R_SUB), lambda i: (0, 0)),
        ],
        out_specs=[pl.BlockSpec((tn, d // 2 + LANES), lambda i: (i, 0)),
                   pl.BlockSpec((2, tn), lambda i: (0, i)),
                   pl.BlockSpec((SEG_PAD, LANES), lambda i: (0, 0))],
        out_shape=[jax.ShapeDtypeStruct((n, d // 2 + LANES), U32),
                   jax.ShapeDtypeStruct((2, n), I32),
                   jax.ShapeDtypeStruct((SEG_PAD, LANES), F32)],
        scratch_shapes=[pltpu.VMEM((SEG_PAD, LANES), F32), pltpu.VMEM((tn, d), BF16)],
        compiler_params=_cparams(("arbitrary",)),
        name="route",
    )(router_b, x1, mod3, g2, wr, tri)


P_TM = 1024
P_ZROWS = 256
ISSUE_UNROLL = 8


def _row_copy(src, src_row, dst, dst_row, sem):
    return pltpu.make_async_copy(src.at[pl.ds(src_row, 1), :], dst.at[pl.ds(dst_row, 1), :], sem)


def _dispatch_kernel(padlo_ref, padhi_ref, slot_ref, h_ref, xs_ref, zrow_ref, sem, zsem):
    tm = h_ref.shape[0]

    def issue(i, carry):
        for j in range(ISSUE_UNROLL):
            r = i * ISSUE_UNROLL + j
            _row_copy(h_ref, r, xs_ref, slot_ref[0, r], sem).start(priority=j % 2)
        return carry

    lax.fori_loop(0, tm // ISSUE_UNROLL, issue, 0)

    @pl.when(pl.program_id(0) == pl.num_programs(0) - 1)
    def _():
        zrow_ref[...] = jnp.zeros_like(zrow_ref)
        zrows = zrow_ref.shape[0]

        def zissue(r, carry):
            _row_copy(zrow_ref, 0, xs_ref, r, zsem).start()
            return carry

        def zwait(r, carry):
            _row_copy(zrow_ref, 0, xs_ref, r, zsem).wait()
            return carry

        for s_id in range(N_SEG):
            lax.fori_loop(padlo_ref[s_id], padhi_ref[s_id], zissue, 0)
            lax.fori_loop(padlo_ref[s_id], padhi_ref[s_id], zwait, 0)

        def chunk(j):
            start = pl.multiple_of(padhi_ref[N_SEG - 1] + j * zrows, zrows)
            return pltpu.make_async_copy(zrow_ref, xs_ref.at[pl.ds(start, zrows), :], zsem)

        n_chunks = (xs_ref.shape[0] - padhi_ref[N_SEG - 1]) // zrows
        lax.fori_loop(0, n_chunks, lambda j, c: (chunk(j).start(), c)[1], 0)
        lax.fori_loop(0, n_chunks, lambda j, c: (chunk(j).wait(), c)[1], 0)

    pltpu.make_async_copy(h_ref, xs_ref.at[pl.ds(0, tm), :], sem).wait()


def _slot_spec(tm, index):
    return pl.BlockSpec((1, tm), lambda i, *_: (0, index(i)), memory_space=pltpu.SMEM)


def _dispatch(hp, slot, pad_lo, pad_hi, cap):
    n, dp = hp.shape
    tm = _tile(n, P_TM)
    grid_spec = pltpu.PrefetchScalarGridSpec(
        num_scalar_prefetch=2,
        grid=(n // tm,),
        in_specs=[_slot_spec(tm, lambda i: i), pl.BlockSpec((tm, dp), lambda i, *_: (i, 0))],
        out_specs=pl.BlockSpec(memory_space=pl.ANY),
        scratch_shapes=[pltpu.VMEM((P_ZROWS, dp), hp.dtype), pltpu.SemaphoreType.DMA(()),
                        pltpu.SemaphoreType.DMA(())],
    )
    return pl.pallas_call(
        _dispatch_kernel,
        grid_spec=grid_spec,
        out_shape=jax.ShapeDtypeStruct((cap, dp), hp.dtype),
        compiler_params=_cparams(("arbitrary",)),
        name="dispatch",
    )(pad_lo, pad_hi, slot, hp)


def _expert_kernel(ea_ref, eb_ref, nu_ref, xs_ref, w1a_ref, w3a_ref, w2a_ref,
                   w1b_ref, w3b_ref, w2b_ref, ys_ref):
    half = ys_ref.shape[1]

    @pl.when(pl.program_id(0) < nu_ref[0])
    def _():
        lo, hi = _unpack_pairs_f32(xs_ref[:, :half])
        x = jnp.concatenate([lo.astype(BF16), hi.astype(BF16)], axis=1)
        gate = lax.bitcast_convert_type(xs_ref[:, half:], F32)

        def ffn(w1_ref, w3_ref, w2_ref):
            a = jnp.dot(x, w1_ref[0], preferred_element_type=F32)
            b = jnp.dot(x, w3_ref[0], preferred_element_type=F32)
            h = (a * jax.nn.sigmoid(a)) * b
            return jnp.dot(h.astype(BF16), w2_ref[0], preferred_element_type=F32)

        y = gate[:, 0:1] * ffn(w1a_ref, w3a_ref, w2a_ref)
        y = y + gate[:, 1:2] * ffn(w1b_ref, w3b_ref, w2b_ref)
        ys_ref[...] = _pack_bf16_pairs(y.astype(BF16))

    @pl.when(pl.program_id(0) >= nu_ref[0])
    def _():
        ys_ref[...] = jnp.zeros_like(ys_ref)


def _experts(xs, block_ea, block_eb, n_used, w1, w3, w2, layer):
    cap, dp = xs.shape
    n_blocks = cap // MOE_BLOCK
    d, de = w1.shape[2:]
    used = lambda i, nu: jnp.minimum(i, nu[0] - 1)
    blk = lambda i, ea, eb, nu: (used(i, nu), 0)
    wa = lambda i, ea, eb, nu: (layer, ea[used(i, nu)], 0, 0)
    wb = lambda i, ea, eb, nu: (layer, eb[used(i, nu)], 0, 0)
    grid_spec = pltpu.PrefetchScalarGridSpec(
        num_scalar_prefetch=3,
        grid=(n_blocks,),
        in_specs=[
            pl.BlockSpec((MOE_BLOCK, dp), blk),
            pl.BlockSpec((None, 1, d, de), wa),
            pl.BlockSpec((None, 1, d, de), wa),
            pl.BlockSpec((None, 1, de, d), wa),
            pl.BlockSpec((None, 1, d, de), wb),
            pl.BlockSpec((None, 1, d, de), wb),
            pl.BlockSpec((None, 1, de, d), wb),
        ],
        out_specs=pl.BlockSpec((MOE_BLOCK, d // 2), lambda i, ea, eb, nu: (i, 0)),
    )
    return pl.pallas_call(
        _expert_kernel,
        grid_spec=grid_spec,
        out_shape=jax.ShapeDtypeStruct((cap, d // 2), U32),
        compiler_params=_cparams(("arbitrary",)),
        name="experts",
    )(block_ea, block_eb, n_used, xs, w1, w3, w2, w1, w3, w2)


C_TM = 512


def _combine_kernel(slot_ref, slotn_ref, x_ref, mod_ref, modn_ref, gn_ref, ys_ref,
                    *rest, final):
    if final:
        o_ref, buf_ref, sem = rest
    else:
        o_ref, hn_ref, buf_ref, sem = rest
    tm = x_ref.shape[0]
    step = pl.program_id(0)
    last = pl.num_programs(0) - 1
    cur = step % 2
    nxt = 1 - cur

    def request(sl_ref, b):
        def body(i, carry):
            for j in range(ISSUE_UNROLL):
                r = i * ISSUE_UNROLL + j
                _row_copy(ys_ref, sl_ref[0, r], buf_ref.at[b], r, sem.at[b]).start(priority=j % 2)
            return carry

        lax.fori_loop(0, tm // ISSUE_UNROLL, body, 0)

    def wait_buffer(b):
        pltpu.make_async_copy(ys_ref.at[pl.ds(0, tm), :], buf_ref.at[b], sem.at[b]).wait()

    @pl.when(step == 0)
    def _():
        request(slot_ref, 0)

    request(slotn_ref, nxt)
    wait_buffer(cur)

    g2 = mod_ref[0, 5:6, :]
    gn = gn_ref[...]
    shift = modn_ref[0, 0:1, :]
    scale = modn_ref[0, 1:2, :]

    def rows_body(i, carry):
        rows = pl.ds(pl.multiple_of(i * NORM_ROWS, NORM_ROWS), NORM_ROWS)
        y = jnp.concatenate(_unpack_pairs_f32(buf_ref[cur, rows, :]), axis=1)
        out = x_ref[rows, :] + g2 * y
        if final:
            ms = jnp.mean(out * out, axis=-1, keepdims=True)
            o_ref[rows, :] = out * lax.rsqrt(ms + EPS) * gn
        else:
            o_ref[rows, :] = out
            hn_ref[rows, :] = _norm_mod_rows(out, gn, shift, scale).astype(hn_ref.dtype)
        return carry

    lax.fori_loop(0, tm // NORM_ROWS, rows_body, 0, unroll=NORM_UNROLL)

    @pl.when(step == last)
    def _():
        wait_buffer(nxt)


def _combine(x1, ys, slot, mod3, modn3, gn, seq, final):
    n, d = x1.shape
    tm = _tile(seq, C_TM)
    per_batch = seq // tm
    row = pl.BlockSpec((tm, d), lambda i, *_: (i, 0))
    mod_spec = pl.BlockSpec((1, N_ADA, d), lambda i, *_: (i // per_batch, 0, 0))
    grid_spec = pltpu.PrefetchScalarGridSpec(
        num_scalar_prefetch=0,
        grid=(n // tm,),
        in_specs=[
            _slot_spec(tm, lambda i: i),
            _slot_spec(tm, lambda i: jnp.minimum(i + 1, n // tm - 1)),
            row,
            mod_spec, mod_spec,
            pl.BlockSpec((1, d), lambda i, *_: (0, 0)),
            pl.BlockSpec(memory_space=pl.ANY),
        ],
        out_specs=row if final else [row, row],
        scratch_shapes=[pltpu.VMEM((2, tm, ys.shape[1]), ys.dtype),
                        pltpu.SemaphoreType.DMA((2,))],
    )
    x_sds = jax.ShapeDtypeStruct((n, d), F32)
    return pl.pallas_call(
        functools.partial(_combine_kernel, final=final),
        grid_spec=grid_spec,
        out_shape=x_sds if final else [x_sds, jax.ShapeDtypeStruct((n, d), BF16)],
        compiler_params=_cparams(("arbitrary",)),
        name="combine",
    )(slot, slot, x1, mod3, modn3, gn, ys)


def _lookup(table, idx):
    table = jnp.asarray(table, I32)
    hit = idx[..., None] == jnp.arange(table.shape[0], dtype=I32)
    return jnp.sum(jnp.where(hit, table, 0), axis=-1).astype(I32)


def _moe(x1, mod3, g2, modn3, gn, wr, router_b, w1, w3, w2, layer, seq, final):
    n, d = x1.shape
    hp, seg_pos, cnt = _route(x1, mod3, g2, wr, router_b, seq)
    counts = cnt[:N_SEG, 0].astype(I32)
    padded = (counts + MOE_BLOCK - 1) // MOE_BLOCK * MOE_BLOCK
    pends = jnp.cumsum(padded).astype(I32)
    pstarts = pends - padded
    n_blocks = -(-n // MOE_BLOCK) + N_SEG
    block_lo = jnp.arange(n_blocks, dtype=I32) * MOE_BLOCK
    block_seg = jnp.minimum(jnp.sum((pends[None, :] <= block_lo[:, None]).astype(I32), axis=1),
                            N_SEG - 1).astype(I32)
    n_used = pends[-1:] // MOE_BLOCK
    slot = (_lookup(pstarts, seg_pos[0]) + seg_pos[1]).reshape(1, n)
    xs = _dispatch(hp, slot, pstarts + counts, pends, n_blocks * MOE_BLOCK)
    ys = _experts(xs, _lookup(SEG_A, block_seg), _lookup(SEG_B, block_seg), n_used,
                  w1, w3, w2, layer)
    return _combine(x1, ys, slot, mod3, modn3, gn, seq, final)


def _fft_split(seq):
    n1_len = 1 << (int(math.log2(seq)) // 2)
    return n1_len, seq // n1_len


def kernel(x, c, positions, ada_w, ada_b, norm1_g, norm2_g, w_in, w_fourier_out, w_attn_out,
           attn_sink, w_out, router_w, router_b, expert_w1, expert_w3, expert_w2, final_norm_g):
    n_batch, seq, d = x.shape
    depth = ada_w.shape[0]
    n1_len, n2_len = _fft_split(seq)
    assert n1_len * n2_len == seq and n1_len % F_SUB == 0 and n2_len % F_SUB == 0
    assert seq % ATT_Q == 0 and seq >= ATT_KW and d == D_MODEL

    mod = _ada_mod(c, ada_w, ada_b).reshape(depth, n_batch, N_ADA, d)
    tabs = _rope_tables(positions)
    consts = _dft_constants(n1_len, n2_len)
    w_in_bf = _reorder_cast_w_in(w_in)
    wr = jnp.pad(router_w, ((0, 0), (0, ROUTER_PAD - N_EXPERTS))).astype(BF16)
    wfo, wao, wout = (w.astype(BF16) for w in (w_fourier_out, w_attn_out, w_out))
    ew1, ew3, ew2 = (w.astype(BF16) for w in (expert_w1, expert_w3, expert_w2))

    x2 = x.reshape(n_batch * seq, d)
    h = _entry_norm(x2, mod[0], norm1_g[0].reshape(1, d), seq)
    for l in range(depth):
        final = l == depth - 1
        u = _in_proj(h, w_in_bf, l)
        yf = _fourier(u, n_batch, seq, consts)
        q_rot, k_rot = _rope(u, tabs)
        att = _attention(q_rot, k_rot, u, attn_sink[l], n_batch, seq)
        x1 = _mix_out(yf, att, u, x2, mod[l], wfo, wao, wout, l, seq)
        gn = (final_norm_g if final else norm1_g[l + 1]).reshape(1, d)
        res = _moe(x1, mod[l], norm2_g[l].reshape(1, d), mod[l if final else l + 1], gn, wr,
                   router_b, ew1, ew3, ew2, l, seq, final)
        if final:
            x2 = res
        else:
            x2, h = res
    return x2.reshape(n_batch, seq, d)
```

```python
import functools
import math

import numpy as np
import jax
import jax.numpy as jnp
from jax import lax
from jax.experimental import pallas as pl
from jax.experimental.pallas import tpu as pltpu

F32 = jnp.float32
BF16 = jnp.bfloat16
I32 = jnp.int32
U32 = jnp.uint32

D_MODEL = 2048
N_ADA = 6
F_GROUPS = 4
F_GDIM = 256
F_WIDTH = F_GROUPS * F_GDIM
N_HEADS = 16
N_KV = 2
HEAD_DIM = 64
HEADS_PER_KV = N_HEADS // N_KV
Q_WIDTH = N_HEADS * HEAD_DIM
KV_WIDTH = N_KV * HEAD_DIM
ROPE_DIM = HEAD_DIM // 4
ROPE_HALF = ROPE_DIM // 2
ROPE_THETA = 500000.0
WINDOW = 128
N_EXPERTS = 16
N_GROUPS = 4
EPG = N_EXPERTS // N_GROUPS
TOP_K = 2
D_EXPERT = 768
MOE_BLOCK = 512
EPS = 1e-6
NEG_INF = -1e30
IN_WIDTH = F_WIDTH + Q_WIDTH + 2 * KV_WIDTH + 2 * D_MODEL

COL_GF = (F_WIDTH + Q_WIDTH) // D_MODEL
COL_GA = COL_GF + 1
COL_K = (F_WIDTH + Q_WIDTH + 2 * D_MODEL) // KV_WIDTH
COL_V = COL_K + 1

LANES = 128
SUBLANES = 8
BF16_ROWS = 16
VMEM_LIMIT = 56 * 1024 * 1024

ATT_Q = 256
ATT_KW = ATT_Q + 2 * WINDOW
ATT_SUB = 128
ATT_SW = ATT_SUB + 2 * WINDOW
ATT_GROUP = 4


def _cparams(sem, vmem=VMEM_LIMIT):
    return pltpu.CompilerParams(dimension_semantics=sem, vmem_limit_bytes=vmem)


def _tile(n, pref):
    t = min(n, pref)
    while n % t:
        t //= 2
    return t


ADA_TN = 768
ADA_ROWS = 32


def _ada_kernel(cb_ref, w_ref, b_ref, o_ref, *, n_batch):
    tn = w_ref.shape[-1]
    n_chunks = w_ref.shape[1] // ADA_ROWS

    def body(i, accs):
        r = pl.multiple_of(i * ADA_ROWS, ADA_ROWS)
        w = w_ref[0, pl.ds(r, ADA_ROWS), :].reshape(ADA_ROWS // SUBLANES, SUBLANES, tn)
        out = []
        for b in range(n_batch):
            c = cb_ref[b, pl.ds(r, ADA_ROWS), :]
            ca = c * jax.nn.sigmoid(c)
            cr = jnp.tile(ca, (1, tn // LANES)).reshape(ADA_ROWS // SUBLANES, SUBLANES, tn)
            out.append(accs[b] + jnp.sum(w * cr, axis=0))
        return tuple(out)

    accs = lax.fori_loop(0, n_chunks, body,
                         tuple(jnp.zeros((SUBLANES, tn), F32) for _ in range(n_batch)))
    for b in range(n_batch):
        o_ref[0, b:b + 1, :] = jnp.sum(accs[b], axis=0, keepdims=True) + b_ref[0]


def _ada_mod(c, ada_w, ada_b):
    n_layers, d, width = ada_w.shape
    n_batch = c.shape[0]
    cb = jnp.broadcast_to(c[:, :, None], (n_batch, d, LANES))
    return pl.pallas_call(
        functools.partial(_ada_kernel, n_batch=n_batch),
        grid=(n_layers, width // ADA_TN),
        in_specs=[
            pl.BlockSpec((n_batch, d, LANES), lambda l, j: (0, 0, 0)),
            pl.BlockSpec((1, d, ADA_TN), lambda l, j: (l, 0, j)),
            pl.BlockSpec((1, 1, ADA_TN), lambda l, j: (l, 0, j)),
        ],
        out_specs=pl.BlockSpec((1, n_batch, ADA_TN), lambda l, j: (l, 0, j)),
        out_shape=jax.ShapeDtypeStruct((n_layers, n_batch, width), F32),
        compiler_params=_cparams(("arbitrary", "arbitrary")),
        name="ada_mod",
    )(cb, ada_w, ada_b.reshape(n_layers, 1, width))


NORM_ROWS = 16
NORM_UNROLL = 8


def _norm_mod_rows(x, g, shift, scale):
    ms = jnp.mean(x * x, axis=-1, keepdims=True)
    y = x * lax.rsqrt(ms + EPS) * g
    return y * (1.0 + scale) + shift


A_TM = 1024
A_TN = 3200


def _a_kernel(h_ref, w_ref, o_ref):
    o_ref[...] = jnp.dot(h_ref[...], w_ref[...], preferred_element_type=F32).astype(o_ref.dtype)


def _in_proj(h, w_bf, layer):
    n, d = h.shape
    width = w_bf.shape[-1]
    tm = _tile(n, A_TM)
    tn = _tile(width, A_TN)
    return pl.pallas_call(
        _a_kernel,
        grid=(n // tm, width // tn),
        in_specs=[
            pl.BlockSpec((tm, d), lambda i, j: (i, 0)),
            pl.BlockSpec((None, d, tn), lambda i, j: (layer, 0, j)),
        ],
        out_specs=pl.BlockSpec((tm, tn), lambda i, j: (i, j)),
        out_shape=jax.ShapeDtypeStruct((n, width), BF16),
        compiler_params=_cparams(("arbitrary", "arbitrary")),
        name="in_proj",
    )(h, w_bf)


W_TILE = 2 * KV_WIDTH


def _cast_kernel(w_ref, o_ref):
    o_ref[...] = w_ref[...].astype(o_ref.dtype)


def _reorder_cast_w_in(w_in):
    depth, d, width = w_in.shape
    kv_tile = (F_WIDTH + Q_WIDTH) // W_TILE
    n_tiles = width // W_TILE
    assert (F_WIDTH + Q_WIDTH) % W_TILE == 0 and width % W_TILE == 0

    def src_tile(j):
        return jnp.where(j < kv_tile, j, jnp.where(j < n_tiles - 1, j + 1, kv_tile))

    return pl.pallas_call(
        _cast_kernel,
        grid=(depth, n_tiles),
        in_specs=[pl.BlockSpec((1, d, W_TILE), lambda l, j: (l, 0, src_tile(j)))],
        out_specs=pl.BlockSpec((1, d, W_TILE), lambda l, j: (l, 0, j)),
        out_shape=jax.ShapeDtypeStruct((depth, d, width), BF16),
        compiler_params=_cparams(("arbitrary", "arbitrary")),
        name="w_in_cast",
    )(w_in)


N_TM = 512


def _norm_rows_to(x_ref, dst_ref, g, shift, scale):
    def body(i, carry):
        r = pl.multiple_of(i * NORM_ROWS, NORM_ROWS)
        h = _norm_mod_rows(x_ref[pl.ds(r, NORM_ROWS), :], g, shift, scale)
        dst_ref[pl.ds(r, NORM_ROWS), :] = h.astype(dst_ref.dtype)
        return carry

    lax.fori_loop(0, x_ref.shape[0] // NORM_ROWS, body, 0, unroll=NORM_UNROLL)


def _norm_kernel(x_ref, mod_ref, g_ref, h_ref):
    _norm_rows_to(x_ref, h_ref, g_ref[...], mod_ref[0, 0:1, :], mod_ref[0, 1:2, :])


def _entry_norm(x2, mod3, g, seq):
    n, d = x2.shape
    tm = _tile(seq, N_TM)
    per_batch = seq // tm
    return pl.pallas_call(
        _norm_kernel,
        grid=(n // tm,),
        in_specs=[
            pl.BlockSpec((tm, d), lambda i: (i, 0)),
            pl.BlockSpec((1, N_ADA, d), lambda i: (i // per_batch, 0, 0)),
            pl.BlockSpec((1, d), lambda i: (0, 0)),
        ],
        out_specs=pl.BlockSpec((tm, d), lambda i: (i, 0)),
        out_shape=jax.ShapeDtypeStruct((n, d), BF16),
        compiler_params=_cparams(("arbitrary",)),
        name="entry_norm",
    )(x2, mod3, g)


ROPE_TM = 512
LOG2E = math.log2(math.e)
Q_SCALE = HEAD_DIM ** -0.5 * LOG2E


def _rope_kernel(q_ref, k_ref, c_ref, sa_ref, sb_ref, qo_ref, ko_ref):
    c = c_ref[...]
    sa = sa_ref[...]
    sb = sb_ref[...]

    def rope(t):
        return (t * c + pltpu.roll(t, LANES - ROPE_HALF, 1) * sa
                + pltpu.roll(t, ROPE_HALF, 1) * sb)

    ko_ref[...] = rope(k_ref[...].astype(F32)).astype(BF16)
    for m in range(Q_WIDTH // LANES):
        q = q_ref[:, m * LANES:(m + 1) * LANES].astype(F32)
        qo_ref[:, m * LANES:(m + 1) * LANES] = (rope(q) * Q_SCALE).astype(BF16)


def _rope(u, tabs):
    n = u.shape[0]
    tm = _tile(n, ROPE_TM)
    tab_spec = pl.BlockSpec((tm, LANES), lambda i: (i, 0))
    return pl.pallas_call(
        _rope_kernel,
        grid=(n // tm,),
        in_specs=[
            pl.BlockSpec((tm, Q_WIDTH), lambda i: (i, F_WIDTH // Q_WIDTH)),
            pl.BlockSpec((tm, KV_WIDTH), lambda i: (i, COL_K)),
            tab_spec, tab_spec, tab_spec,
        ],
        out_specs=[
            pl.BlockSpec((tm, Q_WIDTH), lambda i: (i, 0)),
            pl.BlockSpec((tm, KV_WIDTH), lambda i: (i, 0)),
        ],
        out_shape=[jax.ShapeDtypeStruct((n, Q_WIDTH), BF16),
                   jax.ShapeDtypeStruct((n, KV_WIDTH), BF16)],
        compiler_params=_cparams(("arbitrary",)),
        name="rope",
    )(u, u, *tabs)


def _rope_tables(positions):
    inv = ROPE_THETA ** (-jnp.arange(0, ROPE_DIM, 2, dtype=F32) / ROPE_DIM)
    ang = positions.astype(F32)[..., None] * inv
    cos = jnp.cos(ang).reshape(-1, ROPE_HALF)
    sin = jnp.sin(ang).reshape(-1, ROPE_HALF)
    n = cos.shape[0]
    rest = HEAD_DIM - ROPE_DIM
    ones = jnp.ones((n, rest), F32)
    z_half = jnp.zeros((n, ROPE_HALF), F32)
    z_rest = jnp.zeros((n, rest), F32)
    reps = LANES // HEAD_DIM
    c = jnp.tile(jnp.concatenate([cos, cos, ones], axis=1), (1, reps))
    sa = jnp.tile(jnp.concatenate([-sin, z_half, z_rest], axis=1), (1, reps))
    sb = jnp.tile(jnp.concatenate([z_half, sin, z_rest], axis=1), (1, reps))
    return c, sa, sb


F_SUB = BF16_ROWS


def _dft_constants(n1_len, n2_len):
    cidx = np.arange(F_GDIM)
    ang = 2.0 * np.pi * np.outer(cidx, cidx) / F_GDIM
    wc = np.concatenate([np.cos(ang), -np.sin(ang)], axis=1) / math.sqrt(F_GDIM)
    a = np.arange(n1_len)
    ang1 = 2.0 * np.pi * np.outer(a, a) / n1_len
    fr, fi = np.cos(ang1), -np.sin(ang1)
    fbig = np.block([[fr, -fi], [fi, fr]]) / math.sqrt(n1_len)
    k1 = np.arange(n1_len)[:, None, None]
    k2 = np.arange(n2_len)[None, :, None]
    b = np.arange(n2_len)[None, None, :]
    ang2 = 2.0 * np.pi * (b * k2 / n2_len + b * k1 / (n1_len * n2_len))
    g = np.concatenate([np.cos(ang2), np.sin(ang2)], axis=2) / math.sqrt(n2_len)
    return (jnp.asarray(wc, BF16), jnp.asarray(fbig, BF16), jnp.asarray(g, BF16))


def _f12_kernel(u_ref, wc_ref, fb_ref, t_ref):
    n1_len = u_ref.shape[1]
    x = u_ref[0].reshape(n1_len * F_SUB, F_WIDTH)
    zr, zi = [], []
    for g in range(F_GROUPS):
        r = jnp.dot(x[:, g * F_GDIM:(g + 1) * F_GDIM], wc_ref[...], preferred_element_type=F32)
        zr.append(r[:, :F_GDIM].astype(BF16))
        zi.append(r[:, F_GDIM:].astype(BF16))
    zr = jnp.concatenate(zr, axis=1).reshape(n1_len, F_SUB, F_WIDTH)
    zi = jnp.concatenate(zi, axis=1).reshape(n1_len, F_SUB, F_WIDTH)
    zr = pltpu.einshape("mnc->nmc", zr)
    zi = pltpu.einshape("mnc->nmc", zi)
    for j in range(F_SUB):
        rhs = jnp.concatenate([zr[j], zi[j]], axis=0)
        t = jnp.dot(fb_ref[...], rhs, preferred_element_type=F32)
        t_ref[0, j] = t.astype(BF16).reshape(2, n1_len, F_WIDTH)


def _f3_kernel(t_ref, g_ref, y_ref):
    tr = pltpu.einshape("nkc->knc", t_ref[0, :, 0])
    ti = pltpu.einshape("nkc->knc", t_ref[0, :, 1])
    ys = []
    for j in range(F_SUB):
        rhs = jnp.concatenate([tr[j], ti[j]], axis=0)
        ys.append(jnp.dot(g_ref[j], rhs, preferred_element_type=F32).astype(BF16))
    y_ref[0] = pltpu.einshape("knc->nkc", jnp.stack(ys, axis=0))


def _fourier(u, n_batch, seq, consts):
    wc, fbig, gmat = consts
    n1_len = fbig.shape[0] // 2
    n2_len = seq // n1_len
    u5 = u.reshape(n_batch, n1_len, n2_len, u.shape[-1])
    t = pl.pallas_call(
        _f12_kernel,
        grid=(n_batch, n2_len // F_SUB),
        in_specs=[
            pl.BlockSpec((1, n1_len, F_SUB, F_WIDTH), lambda b, j: (b, 0, j, 0)),
            pl.BlockSpec(wc.shape, lambda b, j: (0, 0)),
            pl.BlockSpec(fbig.shape, lambda b, j: (0, 0)),
        ],
        out_specs=pl.BlockSpec((1, F_SUB, 2, n1_len, F_WIDTH), lambda b, j: (b, j, 0, 0, 0)),
        out_shape=jax.ShapeDtypeStruct((n_batch, n2_len, 2, n1_len, F_WIDTH), BF16),
        compiler_params=_cparams(("arbitrary", "arbitrary")),
        name="fourier_stage1",
    )(u5, wc, fbig)
    y = pl.pallas_call(
        _f3_kernel,
        grid=(n_batch, n1_len // F_SUB),
        in_specs=[
            pl.BlockSpec((1, n2_len, 2, F_SUB, F_WIDTH), lambda b, j: (b, 0, 0, j, 0)),
            pl.BlockSpec((F_SUB, n2_len, 2 * n2_len), lambda b, j: (j, 0, 0)),
        ],
        out_specs=pl.BlockSpec((1, n2_len, F_SUB, F_WIDTH), lambda b, j: (b, 0, j, 0)),
        out_shape=jax.ShapeDtypeStruct((n_batch, n2_len, n1_len, F_WIDTH), BF16),
        compiler_params=_cparams(("arbitrary", "arbitrary")),
        name="fourier_stage2",
    )(t, gmat)
    return y.reshape(n_batch * seq, F_WIDTH)


def _att_kernel(sink_ref, q_ref, k_ref, v_ref, o_ref, kv_ref, *, seq):
    i = pl.program_id(1)
    s0 = i * ATT_Q
    ws = pl.multiple_of(jnp.clip(s0 - WINDOW, 0, seq - ATT_KW), WINDOW)
    off = s0 - ws
    lane = lax.broadcasted_iota(I32, (ATT_KW, LANES), 1)
    low = lane < HEAD_DIM
    for t, ref in enumerate((k_ref, v_ref)):
        w = ref[0, pl.ds(ws, ATT_KW), :].astype(F32)
        w_sw = pltpu.roll(w, HEAD_DIM, 1)
        for h in range(N_KV):
            own, oth = (w, w_sw) if h == 0 else (w_sw, w)
            kv_ref[t, h, 0] = jnp.where(low, own, 0.0).astype(BF16)
            kv_ref[t, h, 1] = jnp.where(low, 0.0, oth).astype(BF16)

    def sub_block(a, carry):
        r0 = pl.multiple_of(a * ATT_SUB, ATT_SUB)
        c0 = pl.multiple_of(
            jnp.clip(((off + r0) // WINDOW) * WINDOW - WINDOW, 0, ATT_KW - ATT_SW), WINDOW)
        row = lax.broadcasted_iota(I32, (ATT_SUB, ATT_SW), 0)
        col = lax.broadcasted_iota(I32, (ATT_SUB, ATT_SW), 1)
        rel = (c0 + col) - (off + r0 + row)
        bias = jnp.where(jnp.abs(rel) <= WINDOW, 0.0, NEG_INF).astype(F32)
        pairs = [(h, m) for h in range(N_KV) for m in range(HEADS_PER_KV // 2)]
        for g0 in range(0, len(pairs), ATT_GROUP):
            group = [(h, m, half) for (h, m) in pairs[g0:g0 + ATT_GROUP] for half in range(2)]
            scores = []
            for h, m, half in group:
                cb = (h * HEADS_PER_KV // 2 + m) * LANES
                q2 = q_ref[pl.ds(r0, ATT_SUB), cb:cb + LANES]
                kk = kv_ref[0, h, half, pl.ds(c0, ATT_SW), :]
                scores.append(lax.dot_general(q2, kk, (((1,), (1,)), ((), ())),
                                              preferred_element_type=F32))
            probs, rdens = [], []
            for (h, m, half), s in zip(group, scores):
                sink = sink_ref[h * HEADS_PER_KV + 2 * m + half] * LOG2E
                s = s + bias
                mx = jnp.maximum(jnp.max(s, axis=-1, keepdims=True), sink)
                p = jnp.exp2(s - mx)
                rdens.append(1.0 / (jnp.sum(p, axis=-1, keepdims=True) + jnp.exp2(sink - mx)))
                probs.append(p.astype(BF16))
            outs = []
            for (h, m, half), p, rden in zip(group, probs, rdens):
                vv = kv_ref[1, h, half, pl.ds(c0, ATT_SW), :]
                outs.append(jnp.dot(p, vv, preferred_element_type=F32) * rden)
            for j, (h, m) in enumerate(pairs[g0:g0 + ATT_GROUP]):
                cb = (h * HEADS_PER_KV // 2 + m) * LANES
                o_ref[pl.ds(r0, ATT_SUB), cb:cb + LANES] = (outs[2 * j] + outs[2 * j + 1]).astype(BF16)
        return carry

    lax.fori_loop(0, ATT_Q // ATT_SUB, sub_block, 0)


def _attention(q_rot, k_rot, u, sink, n_batch, seq):
    n = q_rot.shape[0]
    nq = seq // ATT_Q
    k3 = k_rot.reshape(n_batch, seq, KV_WIDTH)
    u3 = u.reshape(n_batch, seq, u.shape[-1])
    return pl.pallas_call(
        functools.partial(_att_kernel, seq=seq),
        grid=(n_batch, nq),
        in_specs=[
            pl.BlockSpec(memory_space=pltpu.SMEM),
            pl.BlockSpec((ATT_Q, Q_WIDTH), lambda b, i: (b * nq + i, 0)),
            pl.BlockSpec((1, seq, KV_WIDTH), lambda b, i: (b, 0, 0)),
            pl.BlockSpec((1, seq, KV_WIDTH), lambda b, i: (b, 0, COL_V)),
        ],
        out_specs=pl.BlockSpec((ATT_Q, Q_WIDTH), lambda b, i: (b * nq + i, 0)),
        out_shape=jax.ShapeDtypeStruct((n, Q_WIDTH), BF16),
        scratch_shapes=[pltpu.VMEM((2, N_KV, 2, ATT_KW, LANES), BF16)],
        compiler_params=_cparams(("arbitrary", "arbitrary")),
        name="window_attention",
    )(sink, q_rot, k3, u3)


D_TM = 512
ROUTER_PAD = LANES


def _d_kernel(f_ref, a_ref, gf_ref, ga_ref, x_ref, mod_ref, wfo_ref, wao_ref, wout_ref, x1_ref):
    yf = jnp.dot(f_ref[...], wfo_ref[...], preferred_element_type=F32)
    ya = jnp.dot(a_ref[...], wao_ref[...], preferred_element_type=F32)
    merged = (jax.nn.sigmoid(gf_ref[...].astype(F32)) * yf
              + jax.nn.sigmoid(ga_ref[...].astype(F32)) * ya)
    out = jnp.dot(merged.astype(BF16), wout_ref[...], preferred_element_type=F32)
    x1_ref[...] = x_ref[...] + mod_ref[0, 2:3, :] * out


def _layer_weight_spec(w, layer):
    return pl.BlockSpec((None,) + w.shape[1:], lambda i: (layer, 0, 0),
                        pipeline_mode=pl.Buffered(1))


def _mix_out(yf, att, u, x2, mod3, wfo, wao, wout, layer, seq):
    n, d = x2.shape
    tm = _tile(seq, D_TM)
    per_batch = seq // tm
    row = lambda w: pl.BlockSpec((tm, w), lambda i: (i, 0))
    return pl.pallas_call(
        _d_kernel,
        grid=(n // tm,),
        in_specs=[
            row(F_WIDTH), row(Q_WIDTH),
            pl.BlockSpec((tm, d), lambda i: (i, COL_GF)),
            pl.BlockSpec((tm, d), lambda i: (i, COL_GA)),
            row(d),
            pl.BlockSpec((1, N_ADA, d), lambda i: (i // per_batch, 0, 0)),
            _layer_weight_spec(wfo, layer), _layer_weight_spec(wao, layer),
            _layer_weight_spec(wout, layer),
        ],
        out_specs=row(d),
        out_shape=jax.ShapeDtypeStruct((n, d), F32),
        compiler_params=_cparams(("arbitrary",)),
        name="mix_out",
    )(yf, att, u, u, x2, mod3, wfo, wao, wout)


R_TN = 1024
R_SUB = 512
PAIRS = [(a, b) for a in range(EPG) for b in range(a + 1, EPG)]
N_SEG = N_GROUPS * len(PAIRS)
SEG_PAD = 32
SEG_A = [g * EPG + a for g in range(N_GROUPS) for a, _ in PAIRS]
SEG_B = [g * EPG + b for g in range(N_GROUPS) for _, b in PAIRS]
HI_MASK = 0xFFFF0000


def _pack_bf16_pairs(hb):
    bits = lax.bitcast_convert_type(hb.astype(F32), U32)
    half = hb.shape[1] // 2
    return (bits[:, half:] & U32(HI_MASK)) | (bits[:, :half] >> U32(16))


def _unpack_pairs_f32(xu):
    return (lax.bitcast_convert_type(xu << U32(16), F32),
            lax.bitcast_convert_type(xu & U32(HI_MASK), F32))


def _route_kernel(bias_ref, x_ref, mod_ref, g2_ref, wr_ref, tri_ref,
                  hp_ref, sp_ref, cnt_ref, carry_ref, hb_ref):
    @pl.when(pl.program_id(0) == 0)
    def _():
        carry_ref[...] = jnp.zeros_like(carry_ref)

    tn, d = x_ref.shape
    half = d // 2
    g2 = g2_ref[...]
    shift = mod_ref[0, 3:4, :]
    scale = mod_ref[0, 4:5, :]

    def norm_body(i, c):
        r = pl.multiple_of(i * NORM_ROWS, NORM_ROWS)
        hb = _norm_mod_rows(x_ref[pl.ds(r, NORM_ROWS), :], g2, shift, scale).astype(BF16)
        hb_ref[pl.ds(r, NORM_ROWS), :] = hb
        hp_ref[pl.ds(r, NORM_ROWS), :half] = _pack_bf16_pairs(hb)
        return c

    lax.fori_loop(0, tn // NORM_ROWS, norm_body, 0, unroll=NORM_UNROLL)
    logits = jnp.dot(hb_ref[...], wr_ref[...], preferred_element_type=F32)
    lt = logits.T
    sc = [jax.nn.sigmoid(lt[e:e + 1, :]) for e in range(N_EXPERTS)]
    bz = [sc[e] + bias_ref[e] for e in range(N_EXPERTS)]
    one = jnp.ones((1, tn), F32)
    zero = jnp.zeros((1, tn), F32)
    in_top, gscore = [], []
    for g in range(N_GROUPS):
        v = bz[g * EPG:(g + 1) * EPG]
        rank = [zero] * EPG
        for a in range(EPG):
            for b in range(a + 1, EPG):
                a_wins = jnp.where(v[a] >= v[b], one, zero)
                rank[b] = rank[b] + a_wins
                rank[a] = rank[a] + (one - a_wins)
        top = [jnp.where(rank[a] < TOP_K - 0.5, one, zero) for a in range(EPG)]
        in_top += top
        gs = zero
        for a in range(EPG):
            gs = gs + top[a] * v[a]
        gscore.append(gs)
    sel = []
    for g in range(N_GROUPS):
        ok = one
        for other in range(N_GROUPS):
            if other < g:
                ok = ok * jnp.where(gscore[g] > gscore[other], one, zero)
            elif other > g:
                ok = ok * jnp.where(gscore[g] >= gscore[other], one, zero)
        sel += [in_top[g * EPG + a] * ok for a in range(EPG)]

    segsel = [sel[g * EPG + a] * sel[g * EPG + b] for g in range(N_GROUPS) for a, b in PAIRS]
    selm = jnp.concatenate(segsel + [zero] * (SEG_PAD - N_SEG), axis=0)
    carry = carry_ref[:, 0:1]
    cums = []
    for c in range(tn // R_SUB):
        blk = selm[:, c * R_SUB:(c + 1) * R_SUB]
        cums.append(jnp.dot(blk.astype(BF16), tri_ref[...], preferred_element_type=F32) + carry)
        carry = carry + jnp.sum(blk, axis=1, keepdims=True)
    cum = jnp.concatenate(cums, axis=1)
    carry_ref[...] = jnp.broadcast_to(carry, carry_ref.shape)
    cnt_ref[...] = jnp.broadcast_to(carry, cnt_ref.shape)

    seg = zero
    pos = zero
    for s_id in range(N_SEG):
        seg = seg + segsel[s_id] * float(s_id)
        pos = pos + segsel[s_id] * cum[s_id:s_id + 1, :]
    sp_ref[0:1, :] = seg.astype(I32)
    sp_ref[1:2, :] = pos.astype(I32)

    before = zero
    score = [zero, zero]
    for e in range(N_EXPERTS):
        score[0] = score[0] + sel[e] * jnp.where(before < 0.5, one, zero) * sc[e]
        score[1] = score[1] + sel[e] * jnp.where(before >= 0.5, one, zero) * sc[e]
        before = before + sel[e]
    tot = score[0] + score[1]
    rows = lax.broadcasted_iota(I32, (LANES, tn), 0)
    wmat = jnp.where(rows == 0, score[0] / tot, jnp.where(rows == 1, score[1] / tot, 0.0))
    hp_ref[:, half:] = lax.bitcast_convert_type(wmat.T, U32)


def _route(x1, mod3, g2, wr, router_b, seq):
    n, d = x1.shape
    tn = _tile(seq, R_TN)
    per_batch = seq // tn
    tri = jnp.asarray(np.triu(np.ones((R_SUB, R_SUB), np.float32), k=1), BF16)
    return pl.pallas_call(
        _route_kernel,
        grid=(n // tn,),
        in_specs=[
            pl.BlockSpec(memory_space=pltpu.SMEM),
            pl.BlockSpec((tn, d), lambda i: (i, 0)),
            pl.BlockSpec((1, N_ADA, d), lambda i: (i // per_batch, 0, 0)),
            pl.BlockSpec((1, d), lambda i: (0, 0)),
            pl.BlockSpec(wr.shape, lambda i: (0, 0)),
            pl.BlockSpec((R_SUB, R_SUB), lambda i: (0, 0)),
        ],
        out_specs=[pl.BlockSpec((tn, d // 2 + LANES), lambda i: (i, 0)),
                   pl.BlockSpec((2, tn), lambda i: (0, i)),
                   pl.BlockSpec((SEG_PAD, LANES), lambda i: (0, 0))],
        out_shape=[jax.ShapeDtypeStruct((n, d // 2 + LANES), U32),
                   jax.ShapeDtypeStruct((2, n), I32),
                   jax.ShapeDtypeStruct((SEG_PAD, LANES), F32)],
        scratch_shapes=[pltpu.VMEM((SEG_PAD, LANES), F32), pltpu.VMEM((tn, d), BF16)],
        compiler_params=_cparams(("arbitrary",)),
        name="route",
    )(router_b, x1, mod3, g2, wr, tri)


P_TM = 1024
P_ZROWS = 256
ISSUE_UNROLL = 8


def _row_copy(src, src_row, dst, dst_row, sem):
    return pltpu.make_async_copy(src.at[pl.ds(src_row, 1), :], dst.at[pl.ds(dst_row, 1), :], sem)


def _dispatch_kernel(padlo_ref, padhi_ref, slot_ref, h_ref, xs_ref, zrow_ref, sem, zsem):
    tm = h_ref.shape[0]

    def issue(i, carry):
        for j in range(ISSUE_UNROLL):
            r = i * ISSUE_UNROLL + j
            _row_copy(h_ref, r, xs_ref, slot_ref[0, r], sem).start(priority=j % 2)
        return carry

    lax.fori_loop(0, tm // ISSUE_UNROLL, issue, 0)

    @pl.when(pl.program_id(0) == pl.num_programs(0) - 1)
    def _():
        zrow_ref[...] = jnp.zeros_like(zrow_ref)
        zrows = zrow_ref.shape[0]

        def zissue(r, carry):
            _row_copy(zrow_ref, 0, xs_ref, r, zsem).start()
            return carry

        def zwait(r, carry):
            _row_copy(zrow_ref, 0, xs_ref, r, zsem).wait()
            return carry

        def chunked(first, count, rows):
            def chunk(j):
                start = pl.multiple_of(first + j * rows, rows)
                return pltpu.make_async_copy(zrow_ref.at[pl.ds(0, rows), :],
                                             xs_ref.at[pl.ds(start, rows), :], zsem)

            lax.fori_loop(0, count, lambda j, c: (chunk(j).start(), c)[1], 0)
            lax.fori_loop(0, count, lambda j, c: (chunk(j).wait(), c)[1], 0)

        for s_id in range(N_SEG):
            lo = padlo_ref[s_id]
            hi = padhi_ref[s_id]
            aligned = jnp.minimum(hi, (lo + SUBLANES - 1) // SUBLANES * SUBLANES)
            lax.fori_loop(lo, aligned, zissue, 0)
            lax.fori_loop(lo, aligned, zwait, 0)
            chunked(aligned, (hi - aligned) // SUBLANES, SUBLANES)

        chunked(padhi_ref[N_SEG - 1], (xs_ref.shape[0] - padhi_ref[N_SEG - 1]) // zrows, zrows)

    pltpu.make_async_copy(h_ref, xs_ref.at[pl.ds(0, tm), :], sem).wait()


def _slot_spec(tm, index):
    return pl.BlockSpec((1, tm), lambda i, *_: (0, index(i)), memory_space=pltpu.SMEM)


def _dispatch(hp, slot, pad_lo, pad_hi, cap):
    n, dp = hp.shape
    tm = _tile(n, P_TM)
    grid_spec = pltpu.PrefetchScalarGridSpec(
        num_scalar_prefetch=2,
        grid=(n // tm,),
        in_specs=[_slot_spec(tm, lambda i: i), pl.BlockSpec((tm, dp), lambda i, *_: (i, 0))],
        out_specs=pl.BlockSpec(memory_space=pl.ANY),
        scratch_shapes=[pltpu.VMEM((P_ZROWS, dp), hp.dtype), pltpu.SemaphoreType.DMA(()),
                        pltpu.SemaphoreType.DMA(())],
    )
    return pl.pallas_call(
        _dispatch_kernel,
        grid_spec=grid_spec,
        out_shape=jax.ShapeDtypeStruct((cap, dp), hp.dtype),
        compiler_params=_cparams(("arbitrary",)),
        name="dispatch",
    )(pad_lo, pad_hi, slot, hp)


def _expert_kernel(ea_ref, eb_ref, nu_ref, xs_ref, w1a_ref, w3a_ref, w2a_ref,
                   w1b_ref, w3b_ref, w2b_ref, ys_ref):
    half = ys_ref.shape[1]

    @pl.when(pl.program_id(0) < nu_ref[0])
    def _():
        lo, hi = _unpack_pairs_f32(xs_ref[:, :half])
        x = jnp.concatenate([lo.astype(BF16), hi.astype(BF16)], axis=1)
        gate = lax.bitcast_convert_type(xs_ref[:, half:], F32)

        def ffn(w1_ref, w3_ref, w2_ref):
            a = jnp.dot(x, w1_ref[0], preferred_element_type=F32)
            b = jnp.dot(x, w3_ref[0], preferred_element_type=F32)
            h = (a * jax.nn.sigmoid(a)) * b
            return jnp.dot(h.astype(BF16), w2_ref[0], preferred_element_type=F32)

        y = gate[:, 0:1] * ffn(w1a_ref, w3a_ref, w2a_ref)
        y = y + gate[:, 1:2] * ffn(w1b_ref, w3b_ref, w2b_ref)
        ys_ref[...] = _pack_bf16_pairs(y.astype(BF16))

    @pl.when(pl.program_id(0) >= nu_ref[0])
    def _():
        ys_ref[...] = jnp.zeros_like(ys_ref)


def _experts(xs, block_ea, block_eb, n_used, w1, w3, w2, layer):
    cap, dp = xs.shape
    n_blocks = cap // MOE_BLOCK
    d, de = w1.shape[2:]
    used = lambda i, nu: jnp.minimum(i, nu[0] - 1)
    blk = lambda i, ea, eb, nu: (used(i, nu), 0)
    wa = lambda i, ea, eb, nu: (layer, ea[used(i, nu)], 0, 0)
    wb = lambda i, ea, eb, nu: (layer, eb[used(i, nu)], 0, 0)
    grid_spec = pltpu.PrefetchScalarGridSpec(
        num_scalar_prefetch=3,
        grid=(n_blocks,),
        in_specs=[
            pl.BlockSpec((MOE_BLOCK, dp), blk),
            pl.BlockSpec((None, 1, d, de), wa),
            pl.BlockSpec((None, 1, d, de), wa),
            pl.BlockSpec((None, 1, de, d), wa),
            pl.BlockSpec((None, 1, d, de), wb),
            pl.BlockSpec((None, 1, d, de), wb),
            pl.BlockSpec((None, 1, de, d), wb),
        ],
        out_specs=pl.BlockSpec((MOE_BLOCK, d // 2), lambda i, ea, eb, nu: (i, 0)),
    )
    return pl.pallas_call(
        _expert_kernel,
        grid_spec=grid_spec,
        out_shape=jax.ShapeDtypeStruct((cap, d // 2), U32),
        compiler_params=_cparams(("arbitrary",)),
        name="experts",
    )(block_ea, block_eb, n_used, xs, w1, w3, w2, w1, w3, w2)


C_TM = 512


def _combine_kernel(slot_ref, slotn_ref, x_ref, mod_ref, modn_ref, gn_ref, ys_ref,
                    *rest, final):
    if final:
        o_ref, buf_ref, sem = rest
    else:
        o_ref, hn_ref, buf_ref, sem = rest
    tm = x_ref.shape[0]
    step = pl.program_id(0)
    last = pl.num_programs(0) - 1
    cur = step % 2
    nxt = 1 - cur

    def request(sl_ref, b):
        def body(i, carry):
            for j in range(ISSUE_UNROLL):
                r = i * ISSUE_UNROLL + j
                _row_copy(ys_ref, sl_ref[0, r], buf_ref.at[b], r, sem.at[b]).start(priority=j % 2)
            return carry

        lax.fori_loop(0, tm // ISSUE_UNROLL, body, 0)

    def wait_buffer(b):
        pltpu.make_async_copy(ys_ref.at[pl.ds(0, tm), :], buf_ref.at[b], sem.at[b]).wait()

    @pl.when(step == 0)
    def _():
        request(slot_ref, 0)

    request(slotn_ref, nxt)
    wait_buffer(cur)

    g2 = mod_ref[0, 5:6, :]
    gn = gn_ref[...]
    shift = modn_ref[0, 0:1, :]
    scale = modn_ref[0, 1:2, :]

    def rows_body(i, carry):
        rows = pl.ds(pl.multiple_of(i * NORM_ROWS, NORM_ROWS), NORM_ROWS)
        y = jnp.concatenate(_unpack_pairs_f32(buf_ref[cur, rows, :]), axis=1)
        out = x_ref[rows, :] + g2 * y
        if final:
            ms = jnp.mean(out * out, axis=-1, keepdims=True)
            o_ref[rows, :] = out * lax.rsqrt(ms + EPS) * gn
        else:
            o_ref[rows, :] = out
            hn_ref[rows, :] = _norm_mod_rows(out, gn, shift, scale).astype(hn_ref.dtype)
        return carry

    lax.fori_loop(0, tm // NORM_ROWS, rows_body, 0, unroll=NORM_UNROLL)

    @pl.when(step == last)
    def _():
        wait_buffer(nxt)


def _combine(x1, ys, slot, mod3, modn3, gn, seq, final):
    n, d = x1.shape
    tm = _tile(seq, C_TM)
    per_batch = seq // tm
    row = pl.BlockSpec((tm, d), lambda i, *_: (i, 0))
    mod_spec = pl.BlockSpec((1, N_ADA, d), lambda i, *_: (i // per_batch, 0, 0))
    grid_spec = pltpu.PrefetchScalarGridSpec(
        num_scalar_prefetch=0,
        grid=(n // tm,),
        in_specs=[
            _slot_spec(tm, lambda i: i),
            _slot_spec(tm, lambda i: jnp.minimum(i + 1, n // tm - 1)),
            row,
            mod_spec, mod_spec,
            pl.BlockSpec((1, d), lambda i, *_: (0, 0)),
            pl.BlockSpec(memory_space=pl.ANY),
        ],
        out_specs=row if final else [row, row],
        scratch_shapes=[pltpu.VMEM((2, tm, ys.shape[1]), ys.dtype),
                        pltpu.SemaphoreType.DMA((2,))],
    )
    x_sds = jax.ShapeDtypeStruct((n, d), F32)
    return pl.pallas_call(
        functools.partial(_combine_kernel, final=final),
        grid_spec=grid_spec,
        out_shape=x_sds if final else [x_sds, jax.ShapeDtypeStruct((n, d), BF16)],
        compiler_params=_cparams(("arbitrary",)),
        name="combine",
    )(slot, slot, x1, mod3, modn3, gn, ys)


def _lookup(table, idx):
    table = jnp.asarray(table, I32)
    hit = idx[..., None] == jnp.arange(table.shape[0], dtype=I32)
    return jnp.sum(jnp.where(hit, table, 0), axis=-1).astype(I32)


def _moe(x1, mod3, g2, modn3, gn, wr, router_b, w1, w3, w2, layer, seq, final):
    n, d = x1.shape
    hp, seg_pos, cnt = _route(x1, mod3, g2, wr, router_b, seq)
    counts = cnt[:N_SEG, 0].astype(I32)
    padded = (counts + MOE_BLOCK - 1) // MOE_BLOCK * MOE_BLOCK
    pends = jnp.cumsum(padded).astype(I32)
    pstarts = pends - padded
    n_blocks = -(-n // MOE_BLOCK) + N_SEG
    block_lo = jnp.arange(n_blocks, dtype=I32) * MOE_BLOCK
    block_seg = jnp.minimum(jnp.sum((pends[None, :] <= block_lo[:, None]).astype(I32), axis=1),
                            N_SEG - 1).astype(I32)
    n_used = pends[-1:] // MOE_BLOCK
    slot = (_lookup(pstarts, seg_pos[0]) + seg_pos[1]).reshape(1, n)
    xs = _dispatch(hp, slot, pstarts + counts, pends, n_blocks * MOE_BLOCK)
    ys = _experts(xs, _lookup(SEG_A, block_seg), _lookup(SEG_B, block_seg), n_used,
                  w1, w3, w2, layer)
    return _combine(x1, ys, slot, mod3, modn3, gn, seq, final)


def _fft_split(seq):
    n1_len = 1 << (int(math.log2(seq)) // 2)
    return n1_len, seq // n1_len


def kernel(x, c, positions, ada_w, ada_b, norm1_g, norm2_g, w_in, w_fourier_out, w_attn_out,
           attn_sink, w_out, router_w, router_b, expert_w1, expert_w3, expert_w2, final_norm_g):
    n_batch, seq, d = x.shape
    depth = ada_w.shape[0]
    n1_len, n2_len = _fft_split(seq)
    assert n1_len * n2_len == seq and n1_len % F_SUB == 0 and n2_len % F_SUB == 0
    assert seq % ATT_Q == 0 and seq >= ATT_KW and d == D_MODEL

    mod = _ada_mod(c, ada_w, ada_b).reshape(depth, n_batch, N_ADA, d)
    tabs = _rope_tables(positions)
    consts = _dft_constants(n1_len, n2_len)
    w_in_bf = _reorder_cast_w_in(w_in)
    wr = jnp.pad(router_w, ((0, 0), (0, ROUTER_PAD - N_EXPERTS))).astype(BF16)
    wfo, wao, wout = (w.astype(BF16) for w in (w_fourier_out, w_attn_out, w_out))
    ew1, ew3, ew2 = (w.astype(BF16) for w in (expert_w1, expert_w3, expert_w2))

    x2 = x.reshape(n_batch * seq, d)
    h = _entry_norm(x2, mod[0], norm1_g[0].reshape(1, d), seq)
    for l in range(depth):
        final = l == depth - 1
        u = _in_proj(h, w_in_bf, l)
        yf = _fourier(u, n_batch, seq, consts)
        q_rot, k_rot = _rope(u, tabs)
        att = _attention(q_rot, k_rot, u, attn_sink[l], n_batch, seq)
        x1 = _mix_out(yf, att, u, x2, mod[l], wfo, wao, wout, l, seq)
        gn = (final_norm_g if final else norm1_g[l + 1]).reshape(1, d)
        res = _moe(x1, mod[l], norm2_g[l].reshape(1, d), mod[l if final else l + 1], gn, wr,
                   router_b, ew1, ew3, ew2, l, seq, final)
        if final:
            x2 = res
        else:
            x2, h = res
    return x2.reshape(n_batch, seq, d)
```

```python
import functools
import math

import numpy as np
import jax
import jax.numpy as jnp
from jax import lax
from jax.experimental import pallas as pl
from jax.experimental.pallas import tpu as pltpu

F32 = jnp.float32
BF16 = jnp.bfloat16
I32 = jnp.int32
U32 = jnp.uint32

D_MODEL = 2048
N_ADA = 6
F_GROUPS = 4
F_GDIM = 256
F_WIDTH = F_GROUPS * F_GDIM
N_HEADS = 16
N_KV = 2
HEAD_DIM = 64
HEADS_PER_KV = N_HEADS // N_KV
Q_WIDTH = N_HEADS * HEAD_DIM
KV_WIDTH = N_KV * HEAD_DIM
ROPE_DIM = HEAD_DIM // 4
ROPE_HALF = ROPE_DIM // 2
ROPE_THETA = 500000.0
WINDOW = 128
N_EXPERTS = 16
N_GROUPS = 4
EPG = N_EXPERTS // N_GROUPS
TOP_K = 2
D_EXPERT = 768
MOE_BLOCK = 512
EPS = 1e-6
NEG_INF = -1e30
IN_WIDTH = F_WIDTH + Q_WIDTH + 2 * KV_WIDTH + 2 * D_MODEL

COL_GF = (F_WIDTH + Q_WIDTH) // D_MODEL
COL_GA = COL_GF + 1
COL_K = (F_WIDTH + Q_WIDTH + 2 * D_MODEL) // KV_WIDTH
COL_V = COL_K + 1

LANES = 128
SUBLANES = 8
BF16_ROWS = 16
VMEM_LIMIT = 56 * 1024 * 1024

ATT_Q = 256
ATT_KW = ATT_Q + 2 * WINDOW
ATT_SUB = 128
ATT_SW = ATT_SUB + 2 * WINDOW
ATT_GROUP = 4


def _cparams(sem, vmem=VMEM_LIMIT):
    return pltpu.CompilerParams(dimension_semantics=sem, vmem_limit_bytes=vmem)


def _tile(n, pref):
    t = min(n, pref)
    while n % t:
        t //= 2
    return t


ADA_TN = 768
ADA_ROWS = 32


def _ada_kernel(cb_ref, w_ref, b_ref, o_ref, *, n_batch):
    tn = w_ref.shape[-1]
    n_chunks = w_ref.shape[1] // ADA_ROWS

    def body(i, accs):
        r = pl.multiple_of(i * ADA_ROWS, ADA_ROWS)
        w = w_ref[0, pl.ds(r, ADA_ROWS), :].reshape(ADA_ROWS // SUBLANES, SUBLANES, tn)
        out = []
        for b in range(n_batch):
            c = cb_ref[b, pl.ds(r, ADA_ROWS), :]
            ca = c * jax.nn.sigmoid(c)
            cr = jnp.tile(ca, (1, tn // LANES)).reshape(ADA_ROWS // SUBLANES, SUBLANES, tn)
            out.append(accs[b] + jnp.sum(w * cr, axis=0))
        return tuple(out)

    accs = lax.fori_loop(0, n_chunks, body,
                         tuple(jnp.zeros((SUBLANES, tn), F32) for _ in range(n_batch)))
    for b in range(n_batch):
        o_ref[0, b:b + 1, :] = jnp.sum(accs[b], axis=0, keepdims=True) + b_ref[0]


def _ada_mod(c, ada_w, ada_b):
    n_layers, d, width = ada_w.shape
    n_batch = c.shape[0]
    cb = jnp.broadcast_to(c[:, :, None], (n_batch, d, LANES))
    return pl.pallas_call(
        functools.partial(_ada_kernel, n_batch=n_batch),
        grid=(n_layers, width // ADA_TN),
        in_specs=[
            pl.BlockSpec((n_batch, d, LANES), lambda l, j: (0, 0, 0)),
            pl.BlockSpec((1, d, ADA_TN), lambda l, j: (l, 0, j)),
            pl.BlockSpec((1, 1, ADA_TN), lambda l, j: (l, 0, j)),
        ],
        out_specs=pl.BlockSpec((1, n_batch, ADA_TN), lambda l, j: (l, 0, j)),
        out_shape=jax.ShapeDtypeStruct((n_layers, n_batch, width), F32),
        compiler_params=_cparams(("arbitrary", "arbitrary")),
        name="ada_mod",
    )(cb, ada_w, ada_b.reshape(n_layers, 1, width))


NORM_ROWS = 16
NORM_UNROLL = 8


def _norm_mod_rows(x, g, shift, scale):
    ms = jnp.mean(x * x, axis=-1, keepdims=True)
    y = x * lax.rsqrt(ms + EPS) * g
    return y * (1.0 + scale) + shift


A_TM = 1024
A_TN = 1280


def _a_kernel(h_ref, w_ref, c1_ref, c3_ref, c2_ref, o_ref, b1_ref, b3_ref, b2_ref):
    o_ref[...] = jnp.dot(h_ref[...], w_ref[...], preferred_element_type=F32).astype(o_ref.dtype)
    for src, dst in ((c1_ref, b1_ref), (c3_ref, b3_ref), (c2_ref, b2_ref)):
        dst[...] = src[...].astype(dst.dtype)


def _in_proj(h, w_bf, layer, expert_ws):
    n, d = h.shape
    width = w_bf.shape[-1]
    tm = _tile(n, A_TM)
    tn = _tile(width, A_TN)
    nj = width // tn
    steps = (n // tm) * nj
    n_chunks = 1 << (steps.bit_length() - 1)
    chunk_id = lambda i, j: jnp.minimum(i * nj + j, n_chunks - 1)
    side_in, side_out, side_shapes, flats = [], [], [], []
    for w in expert_ws:
        depth, n_exp, rows, cols = w.shape
        per_layer = n_exp * rows
        chunk = per_layer // n_chunks
        assert per_layer % n_chunks == 0 and chunk % BF16_ROWS == 0
        flats.append(w.reshape(depth * per_layer, cols))
        side_in.append(
            pl.BlockSpec((chunk, cols), lambda i, j: (layer * n_chunks + chunk_id(i, j), 0)))
        side_out.append(pl.BlockSpec((chunk, cols), lambda i, j: (chunk_id(i, j), 0)))
        side_shapes.append(jax.ShapeDtypeStruct((per_layer, cols), BF16))
    outs = pl.pallas_call(
        _a_kernel,
        grid=(n // tm, nj),
        in_specs=[
            pl.BlockSpec((tm, d), lambda i, j: (i, 0)),
            pl.BlockSpec((None, d, tn), lambda i, j: (layer, 0, j)),
        ] + side_in,
        out_specs=[pl.BlockSpec((tm, tn), lambda i, j: (i, j))] + side_out,
        out_shape=[jax.ShapeDtypeStruct((n, width), BF16)] + side_shapes,
        compiler_params=_cparams(("arbitrary", "arbitrary")),
        name="in_proj",
    )(h, w_bf, *flats)
    casts = [o.reshape(w.shape[1:]) for o, w in zip(outs[1:], expert_ws)]
    return outs[0], casts


W_TILE = 2 * KV_WIDTH


def _cast_kernel(w_ref, o_ref):
    o_ref[...] = w_ref[...].astype(o_ref.dtype)


def _reorder_cast_w_in(w_in):
    depth, d, width = w_in.shape
    kv_tile = (F_WIDTH + Q_WIDTH) // W_TILE
    n_tiles = width // W_TILE
    assert (F_WIDTH + Q_WIDTH) % W_TILE == 0 and width % W_TILE == 0

    def src_tile(j):
        return jnp.where(j < kv_tile, j, jnp.where(j < n_tiles - 1, j + 1, kv_tile))

    return pl.pallas_call(
        _cast_kernel,
        grid=(depth, n_tiles),
        in_specs=[pl.BlockSpec((1, d, W_TILE), lambda l, j: (l, 0, src_tile(j)))],
        out_specs=pl.BlockSpec((1, d, W_TILE), lambda l, j: (l, 0, j)),
        out_shape=jax.ShapeDtypeStruct((depth, d, width), BF16),
        compiler_params=_cparams(("arbitrary", "arbitrary")),
        name="w_in_cast",
    )(w_in)


N_TM = 512


def _norm_rows_to(x_ref, dst_ref, g, shift, scale):
    def body(i, carry):
        r = pl.multiple_of(i * NORM_ROWS, NORM_ROWS)
        h = _norm_mod_rows(x_ref[pl.ds(r, NORM_ROWS), :], g, shift, scale)
        dst_ref[pl.ds(r, NORM_ROWS), :] = h.astype(dst_ref.dtype)
        return carry

    lax.fori_loop(0, x_ref.shape[0] // NORM_ROWS, body, 0, unroll=NORM_UNROLL)


def _norm_kernel(x_ref, mod_ref, g_ref, h_ref):
    _norm_rows_to(x_ref, h_ref, g_ref[...], mod_ref[0, 0:1, :], mod_ref[0, 1:2, :])


def _entry_norm(x2, mod3, g, seq):
    n, d = x2.shape
    tm = _tile(seq, N_TM)
    per_batch = seq // tm
    return pl.pallas_call(
        _norm_kernel,
        grid=(n // tm,),
        in_specs=[
            pl.BlockSpec((tm, d), lambda i: (i, 0)),
            pl.BlockSpec((1, N_ADA, d), lambda i: (i // per_batch, 0, 0)),
            pl.BlockSpec((1, d), lambda i: (0, 0)),
        ],
        out_specs=pl.BlockSpec((tm, d), lambda i: (i, 0)),
        out_shape=jax.ShapeDtypeStruct((n, d), BF16),
        compiler_params=_cparams(("arbitrary",)),
        name="entry_norm",
    )(x2, mod3, g)


ROPE_TM = 512
LOG2E = math.log2(math.e)
Q_SCALE = HEAD_DIM ** -0.5 * LOG2E


def _rope_kernel(q_ref, k_ref, c_ref, sa_ref, sb_ref, qo_ref, ko_ref):
    c = c_ref[...]
    sa = sa_ref[...]
    sb = sb_ref[...]

    def rope(t):
        return (t * c + pltpu.roll(t, LANES - ROPE_HALF, 1) * sa
                + pltpu.roll(t, ROPE_HALF, 1) * sb)

    ko_ref[...] = rope(k_ref[...].astype(F32)).astype(BF16)
    for m in range(Q_WIDTH // LANES):
        q = q_ref[:, m * LANES:(m + 1) * LANES].astype(F32)
        qo_ref[:, m * LANES:(m + 1) * LANES] = (rope(q) * Q_SCALE).astype(BF16)


def _rope(u, tabs):
    n = u.shape[0]
    tm = _tile(n, ROPE_TM)
    tab_spec = pl.BlockSpec((tm, LANES), lambda i: (i, 0))
    return pl.pallas_call(
        _rope_kernel,
        grid=(n // tm,),
        in_specs=[
            pl.BlockSpec((tm, Q_WIDTH), lambda i: (i, F_WIDTH // Q_WIDTH)),
            pl.BlockSpec((tm, KV_WIDTH), lambda i: (i, COL_K)),
            tab_spec, tab_spec, tab_spec,
        ],
        out_specs=[
            pl.BlockSpec((tm, Q_WIDTH), lambda i: (i, 0)),
            pl.BlockSpec((tm, KV_WIDTH), lambda i: (i, 0)),
        ],
        out_shape=[jax.ShapeDtypeStruct((n, Q_WIDTH), BF16),
                   jax.ShapeDtypeStruct((n, KV_WIDTH), BF16)],
        compiler_params=_cparams(("arbitrary",)),
        name="rope",
    )(u, u, *tabs)


def _rope_tables(positions):
    inv = ROPE_THETA ** (-jnp.arange(0, ROPE_DIM, 2, dtype=F32) / ROPE_DIM)
    ang = positions.astype(F32)[..., None] * inv
    cos = jnp.cos(ang).reshape(-1, ROPE_HALF)
    sin = jnp.sin(ang).reshape(-1, ROPE_HALF)
    n = cos.shape[0]
    rest = HEAD_DIM - ROPE_DIM
    ones = jnp.ones((n, rest), F32)
    z_half = jnp.zeros((n, ROPE_HALF), F32)
    z_rest = jnp.zeros((n, rest), F32)
    reps = LANES // HEAD_DIM
    c = jnp.tile(jnp.concatenate([cos, cos, ones], axis=1), (1, reps))
    sa = jnp.tile(jnp.concatenate([-sin, z_half, z_rest], axis=1), (1, reps))
    sb = jnp.tile(jnp.concatenate([z_half, sin, z_rest], axis=1), (1, reps))
    return c, sa, sb


F_SUB = BF16_ROWS


def _dft_constants(n1_len, n2_len):
    cidx = np.arange(F_GDIM)
    ang = 2.0 * np.pi * np.outer(cidx, cidx) / F_GDIM
    wc = np.concatenate([np.cos(ang), -np.sin(ang)], axis=1) / math.sqrt(F_GDIM)
    a = np.arange(n1_len)
    ang1 = 2.0 * np.pi * np.outer(a, a) / n1_len
    fr, fi = np.cos(ang1), -np.sin(ang1)
    fbig = np.block([[fr, -fi], [fi, fr]]) / math.sqrt(n1_len)
    k1 = np.arange(n1_len)[:, None, None]
    k2 = np.arange(n2_len)[None, :, None]
    b = np.arange(n2_len)[None, None, :]
    ang2 = 2.0 * np.pi * (b * k2 / n2_len + b * k1 / (n1_len * n2_len))
    g = np.concatenate([np.cos(ang2), np.sin(ang2)], axis=2) / math.sqrt(n2_len)
    return (jnp.asarray(wc, BF16), jnp.asarray(fbig, BF16), jnp.asarray(g, BF16))


def _f12_kernel(u_ref, wc_ref, fb_ref, t_ref):
    n1_len = u_ref.shape[1]
    x = u_ref[0].reshape(n1_len * F_SUB, F_WIDTH)
    zr, zi = [], []
    for g in range(F_GROUPS):
        r = jnp.dot(x[:, g * F_GDIM:(g + 1) * F_GDIM], wc_ref[...], preferred_element_type=F32)
        zr.append(r[:, :F_GDIM].astype(BF16))
        zi.append(r[:, F_GDIM:].astype(BF16))
    zr = jnp.concatenate(zr, axis=1).reshape(n1_len, F_SUB, F_WIDTH)
    zi = jnp.concatenate(zi, axis=1).reshape(n1_len, F_SUB, F_WIDTH)
    zr = pltpu.einshape("mnc->nmc", zr)
    zi = pltpu.einshape("mnc->nmc", zi)
    for j in range(F_SUB):
        rhs = jnp.concatenate([zr[j], zi[j]], axis=0)
        t = jnp.dot(fb_ref[...], rhs, preferred_element_type=F32)
        t_ref[0, j] = t.astype(BF16).reshape(2, n1_len, F_WIDTH)


def _f3_kernel(t_ref, g_ref, y_ref):
    tr = pltpu.einshape("nkc->knc", t_ref[0, :, 0])
    ti = pltpu.einshape("nkc->knc", t_ref[0, :, 1])
    ys = []
    for j in range(F_SUB):
        rhs = jnp.concatenate([tr[j], ti[j]], axis=0)
        ys.append(jnp.dot(g_ref[j], rhs, preferred_element_type=F32).astype(BF16))
    y_ref[0] = pltpu.einshape("knc->nkc", jnp.stack(ys, axis=0))


def _fourier(u, n_batch, seq, consts):
    wc, fbig, gmat = consts
    n1_len = fbig.shape[0] // 2
    n2_len = seq // n1_len
    u5 = u.reshape(n_batch, n1_len, n2_len, u.shape[-1])
    t = pl.pallas_call(
        _f12_kernel,
        grid=(n_batch, n2_len // F_SUB),
        in_specs=[
            pl.BlockSpec((1, n1_len, F_SUB, F_WIDTH), lambda b, j: (b, 0, j, 0)),
            pl.BlockSpec(wc.shape, lambda b, j: (0, 0)),
            pl.BlockSpec(fbig.shape, lambda b, j: (0, 0)),
        ],
        out_specs=pl.BlockSpec((1, F_SUB, 2, n1_len, F_WIDTH), lambda b, j: (b, j, 0, 0, 0)),
        out_shape=jax.ShapeDtypeStruct((n_batch, n2_len, 2, n1_len, F_WIDTH), BF16),
        compiler_params=_cparams(("arbitrary", "arbitrary")),
        name="fourier_stage1",
    )(u5, wc, fbig)
    y = pl.pallas_call(
        _f3_kernel,
        grid=(n_batch, n1_len // F_SUB),
        in_specs=[
            pl.BlockSpec((1, n2_len, 2, F_SUB, F_WIDTH), lambda b, j: (b, 0, 0, j, 0)),
            pl.BlockSpec((F_SUB, n2_len, 2 * n2_len), lambda b, j: (j, 0, 0)),
        ],
        out_specs=pl.BlockSpec((1, n2_len, F_SUB, F_WIDTH), lambda b, j: (b, 0, j, 0)),
        out_shape=jax.ShapeDtypeStruct((n_batch, n2_len, n1_len, F_WIDTH), BF16),
        compiler_params=_cparams(("arbitrary", "arbitrary")),
        name="fourier_stage2",
    )(t, gmat)
    return y.reshape(n_batch * seq, F_WIDTH)


def _att_kernel(sink_ref, q_ref, k_ref, v_ref, o_ref, kv_ref, *, seq):
    i = pl.program_id(1)
    s0 = i * ATT_Q
    ws = pl.multiple_of(jnp.clip(s0 - WINDOW, 0, seq - ATT_KW), WINDOW)
    off = s0 - ws
    lane = lax.broadcasted_iota(I32, (ATT_KW, LANES), 1)
    low = lane < HEAD_DIM
    for t, ref in enumerate((k_ref, v_ref)):
        w = ref[0, pl.ds(ws, ATT_KW), :].astype(F32)
        w_sw = pltpu.roll(w, HEAD_DIM, 1)
        for h in range(N_KV):
            own, oth = (w, w_sw) if h == 0 else (w_sw, w)
            kv_ref[t, h, 0] = jnp.where(low, own, 0.0).astype(BF16)
            kv_ref[t, h, 1] = jnp.where(low, 0.0, oth).astype(BF16)

    def sub_block(a, carry):
        r0 = pl.multiple_of(a * ATT_SUB, ATT_SUB)
        c0 = pl.multiple_of(
            jnp.clip(((off + r0) // WINDOW) * WINDOW - WINDOW, 0, ATT_KW - ATT_SW), WINDOW)
        row = lax.broadcasted_iota(I32, (ATT_SUB, ATT_SW), 0)
        col = lax.broadcasted_iota(I32, (ATT_SUB, ATT_SW), 1)
        rel = (c0 + col) - (off + r0 + row)
        bias = jnp.where(jnp.abs(rel) <= WINDOW, 0.0, NEG_INF).astype(F32)
        pairs = [(h, m) for h in range(N_KV) for m in range(HEADS_PER_KV // 2)]
        for g0 in range(0, len(pairs), ATT_GROUP):
            group = [(h, m, half) for (h, m) in pairs[g0:g0 + ATT_GROUP] for half in range(2)]
            scores = []
            for h, m, half in group:
                cb = (h * HEADS_PER_KV // 2 + m) * LANES
                q2 = q_ref[pl.ds(r0, ATT_SUB), cb:cb + LANES]
                kk = kv_ref[0, h, half, pl.ds(c0, ATT_SW), :]
                scores.append(lax.dot_general(q2, kk, (((1,), (1,)), ((), ())),
                                              preferred_element_type=F32))
            probs, rdens = [], []
            for (h, m, half), s in zip(group, scores):
                sink = sink_ref[h * HEADS_PER_KV + 2 * m + half] * LOG2E
                s = s + bias
                mx = jnp.maximum(jnp.max(s, axis=-1, keepdims=True), sink)
                p = jnp.exp2(s - mx)
                rdens.append(1.0 / (jnp.sum(p, axis=-1, keepdims=True) + jnp.exp2(sink - mx)))
                probs.append(p.astype(BF16))
            outs = []
            for (h, m, half), p, rden in zip(group, probs, rdens):
                vv = kv_ref[1, h, half, pl.ds(c0, ATT_SW), :]
                outs.append(jnp.dot(p, vv, preferred_element_type=F32) * rden)
            for j, (h, m) in enumerate(pairs[g0:g0 + ATT_GROUP]):
                cb = (h * HEADS_PER_KV // 2 + m) * LANES
                o_ref[pl.ds(r0, ATT_SUB), cb:cb + LANES] = (outs[2 * j] + outs[2 * j + 1]).astype(BF16)
        return carry

    lax.fori_loop(0, ATT_Q // ATT_SUB, sub_block, 0)


def _attention(q_rot, k_rot, u, sink, n_batch, seq):
    n = q_rot.shape[0]
    nq = seq // ATT_Q
    k3 = k_rot.reshape(n_batch, seq, KV_WIDTH)
    u3 = u.reshape(n_batch, seq, u.shape[-1])
    return pl.pallas_call(
        functools.partial(_att_kernel, seq=seq),
        grid=(n_batch, nq),
        in_specs=[
            pl.BlockSpec(memory_space=pltpu.SMEM),
            pl.BlockSpec((ATT_Q, Q_WIDTH), lambda b, i: (b * nq + i, 0)),
            pl.BlockSpec((1, seq, KV_WIDTH), lambda b, i: (b, 0, 0)),
            pl.BlockSpec((1, seq, KV_WIDTH), lambda b, i: (b, 0, COL_V)),
        ],
        out_specs=pl.BlockSpec((ATT_Q, Q_WIDTH), lambda b, i: (b * nq + i, 0)),
        out_shape=jax.ShapeDtypeStruct((n, Q_WIDTH), BF16),
        scratch_shapes=[pltpu.VMEM((2, N_KV, 2, ATT_KW, LANES), BF16)],
        compiler_params=_cparams(("arbitrary", "arbitrary")),
        name="window_attention",
    )(sink, q_rot, k3, u3)


D_TM = 512
ROUTER_PAD = LANES


def _d_kernel(f_ref, a_ref, gf_ref, ga_ref, x_ref, mod_ref, wfo_ref, wao_ref, wout_ref, x1_ref):
    yf = jnp.dot(f_ref[...], wfo_ref[...], preferred_element_type=F32)
    ya = jnp.dot(a_ref[...], wao_ref[...], preferred_element_type=F32)
    merged = (jax.nn.sigmoid(gf_ref[...].astype(F32)) * yf
              + jax.nn.sigmoid(ga_ref[...].astype(F32)) * ya)
    out = jnp.dot(merged.astype(BF16), wout_ref[...], preferred_element_type=F32)
    x1_ref[...] = x_ref[...] + mod_ref[0, 2:3, :] * out


def _layer_weight_spec(w, layer):
    return pl.BlockSpec((None,) + w.shape[1:], lambda i: (layer, 0, 0),
                        pipeline_mode=pl.Buffered(1))


def _mix_out(yf, att, u, x2, mod3, wfo, wao, wout, layer, seq):
    n, d = x2.shape
    tm = _tile(seq, D_TM)
    per_batch = seq // tm
    row = lambda w: pl.BlockSpec((tm, w), lambda i: (i, 0))
    return pl.pallas_call(
        _d_kernel,
        grid=(n // tm,),
        in_specs=[
            row(F_WIDTH), row(Q_WIDTH),
            pl.BlockSpec((tm, d), lambda i: (i, COL_GF)),
            pl.BlockSpec((tm, d), lambda i: (i, COL_GA)),
            row(d),
            pl.BlockSpec((1, N_ADA, d), lambda i: (i // per_batch, 0, 0)),
            _layer_weight_spec(wfo, layer), _layer_weight_spec(wao, layer),
            _layer_weight_spec(wout, layer),
        ],
        out_specs=row(d),
        out_shape=jax.ShapeDtypeStruct((n, d), F32),
        compiler_params=_cparams(("arbitrary",)),
        name="mix_out",
    )(yf, att, u, u, x2, mod3, wfo, wao, wout)


R_TN = 1024
R_SUB = 512
PAIRS = [(a, b) for a in range(EPG) for b in range(a + 1, EPG)]
N_SEG = N_GROUPS * len(PAIRS)
SEG_PAD = 32
SEG_A = [g * EPG + a for g in range(N_GROUPS) for a, _ in PAIRS]
SEG_B = [g * EPG + b for g in range(N_GROUPS) for _, b in PAIRS]
HI_MASK = 0xFFFF0000


def _pack_bf16_pairs(hb):
    bits = lax.bitcast_convert_type(hb.astype(F32), U32)
    half = hb.shape[1] // 2
    return (bits[:, half:] & U32(HI_MASK)) | (bits[:, :half] >> U32(16))


def _unpack_pairs_f32(xu):
    return (lax.bitcast_convert_type(xu << U32(16), F32),
            lax.bitcast_convert_type(xu & U32(HI_MASK), F32))


def _route_kernel(bias_ref, x_ref, mod_ref, g2_ref, wr_ref, tri_ref,
                  hp_ref, sp_ref, cnt_ref, carry_ref, hb_ref):
    @pl.when(pl.program_id(0) == 0)
    def _():
        carry_ref[...] = jnp.zeros_like(carry_ref)

    tn, d = x_ref.shape
    half = d // 2
    g2 = g2_ref[...]
    shift = mod_ref[0, 3:4, :]
    scale = mod_ref[0, 4:5, :]

    def norm_body(i, c):
        r = pl.multiple_of(i * NORM_ROWS, NORM_ROWS)
        hb = _norm_mod_rows(x_ref[pl.ds(r, NORM_ROWS), :], g2, shift, scale).astype(BF16)
        hb_ref[pl.ds(r, NORM_ROWS), :] = hb
        hp_ref[pl.ds(r, NORM_ROWS), :half] = _pack_bf16_pairs(hb)
        return c

    lax.fori_loop(0, tn // NORM_ROWS, norm_body, 0, unroll=NORM_UNROLL)
    logits = jnp.dot(hb_ref[...], wr_ref[...], preferred_element_type=F32)
    lt = logits.T
    sc = [jax.nn.sigmoid(lt[e:e + 1, :]) for e in range(N_EXPERTS)]
    bz = [sc[e] + bias_ref[e] for e in range(N_EXPERTS)]
    one = jnp.ones((1, tn), F32)
    zero = jnp.zeros((1, tn), F32)
    in_top, gscore = [], []
    for g in range(N_GROUPS):
        v = bz[g * EPG:(g + 1) * EPG]
        rank = [zero] * EPG
        for a in range(EPG):
            for b in range(a + 1, EPG):
                a_wins = jnp.where(v[a] >= v[b], one, zero)
                rank[b] = rank[b] + a_wins
                rank[a] = rank[a] + (one - a_wins)
        top = [jnp.where(rank[a] < TOP_K - 0.5, one, zero) for a in range(EPG)]
        in_top += top
        gs = zero
        for a in range(EPG):
            gs = gs + top[a] * v[a]
        gscore.append(gs)
    sel = []
    for g in range(N_GROUPS):
        ok = one
        for other in range(N_GROUPS):
            if other < g:
                ok = ok * jnp.where(gscore[g] > gscore[other], one, zero)
            elif other > g:
                ok = ok * jnp.where(gscore[g] >= gscore[other], one, zero)
        sel += [in_top[g * EPG + a] * ok for a in range(EPG)]

    segsel = [sel[g * EPG + a] * sel[g * EPG + b] for g in range(N_GROUPS) for a, b in PAIRS]
    selm = jnp.concatenate(segsel + [zero] * (SEG_PAD - N_SEG), axis=0)
    carry = carry_ref[:, 0:1]
    cums = []
    for c in range(tn // R_SUB):
        blk = selm[:, c * R_SUB:(c + 1) * R_SUB]
        cums.append(jnp.dot(blk.astype(BF16), tri_ref[...], preferred_element_type=F32) + carry)
        carry = carry + jnp.sum(blk, axis=1, keepdims=True)
    cum = jnp.concatenate(cums, axis=1)
    carry_ref[...] = jnp.broadcast_to(carry, carry_ref.shape)
    cnt_ref[...] = jnp.broadcast_to(carry, cnt_ref.shape)

    seg = zero
    pos = zero
    for s_id in range(N_SEG):
        seg = seg + segsel[s_id] * float(s_id)
        pos = pos + segsel[s_id] * cum[s_id:s_id + 1, :]
    sp_ref[0:1, :] = seg.astype(I32)
    sp_ref[1:2, :] = pos.astype(I32)

    before = zero
    score = [zero, zero]
    for e in range(N_EXPERTS):
        score[0] = score[0] + sel[e] * jnp.where(before < 0.5, one, zero) * sc[e]
        score[1] = score[1] + sel[e] * jnp.where(before >= 0.5, one, zero) * sc[e]
        before = before + sel[e]
    tot = score[0] + score[1]
    rows = lax.broadcasted_iota(I32, (LANES, tn), 0)
    wmat = jnp.where(rows == 0, score[0] / tot, jnp.where(rows == 1, score[1] / tot, 0.0))
    hp_ref[:, half:] = lax.bitcast_convert_type(wmat.T, U32)


def _route(x1, mod3, g2, wr, router_b, seq):
    n, d = x1.shape
    tn = _tile(seq, R_TN)
    per_batch = seq // tn
    tri = jnp.asarray(np.triu(np.ones((R_SUB, R_SUB), np.float32), k=1), BF16)
    return pl.pallas_call(
        _route_kernel,
        grid=(n // tn,),
        in_specs=[
            pl.BlockSpec(memory_space=pltpu.SMEM),
            pl.BlockSpec((tn, d), lambda i: (i, 0)),
            pl.BlockSpec((1, N_ADA, d), lambda i: (i // per_batch, 0, 0)),
            pl.BlockSpec((1, d), lambda i: (0, 0)),
            pl.BlockSpec(wr.shape, lambda i: (0, 0)),
            pl.BlockSpec((R_SUB, R_SUB), lambda i: (0, 0)),
        ],
        out_specs=[pl.BlockSpec((tn, d // 2 + LANES), lambda i: (i, 0)),
                   pl.BlockSpec((2, tn), lambda i: (0, i)),
                   pl.BlockSpec((SEG_PAD, LANES), lambda i: (0, 0))],
        out_shape=[jax.ShapeDtypeStruct((n, d // 2 + LANES), U32),
                   jax.ShapeDtypeStruct((2, n), I32),
                   jax.ShapeDtypeStruct((SEG_PAD, LANES), F32)],
        scratch_shapes=[pltpu.VMEM((SEG_PAD, LANES), F32), pltpu.VMEM((tn, d), BF16)],
        compiler_params=_cparams(("arbitrary",)),
        name="route",
    )(router_b, x1, mod3, g2, wr, tri)


P_TM = 1024
P_ZROWS = 256
ISSUE_UNROLL = 8


def _row_copy(src, src_row, dst, dst_row, sem):
    return pltpu.make_async_copy(src.at[pl.ds(src_row, 1), :], dst.at[pl.ds(dst_row, 1), :], sem)


def _dispatch_kernel(padlo_ref, padhi_ref, slot_ref, h_ref, xs_ref, zrow_ref, sem, zsem):
    tm = h_ref.shape[0]

    def issue(i, carry):
        for j in range(ISSUE_UNROLL):
            r = i * ISSUE_UNROLL + j
            _row_copy(h_ref, r, xs_ref, slot_ref[0, r], sem).start(priority=j % 2)
        return carry

    lax.fori_loop(0, tm // ISSUE_UNROLL, issue, 0)

    @pl.when(pl.program_id(0) == pl.num_programs(0) - 1)
    def _():
        zrow_ref[...] = jnp.zeros_like(zrow_ref)
        zrows = zrow_ref.shape[0]

        def zissue(r, carry):
            _row_copy(zrow_ref, 0, xs_ref, r, zsem).start()
            return carry

        def zwait(r, carry):
            _row_copy(zrow_ref, 0, xs_ref, r, zsem).wait()
            return carry

        def chunked(first, count, rows):
            def chunk(j):
                start = pl.multiple_of(first + j * rows, rows)
                return pltpu.make_async_copy(zrow_ref.at[pl.ds(0, rows), :],
                                             xs_ref.at[pl.ds(start, rows), :], zsem)

            lax.fori_loop(0, count, lambda j, c: (chunk(j).start(), c)[1], 0)
            lax.fori_loop(0, count, lambda j, c: (chunk(j).wait(), c)[1], 0)

        for s_id in range(N_SEG):
            lo = padlo_ref[s_id]
            hi = padhi_ref[s_id]
            aligned = jnp.minimum(hi, (lo + SUBLANES - 1) // SUBLANES * SUBLANES)
            lax.fori_loop(lo, aligned, zissue, 0)
            lax.fori_loop(lo, aligned, zwait, 0)
            chunked(aligned, (hi - aligned) // SUBLANES, SUBLANES)

        chunked(padhi_ref[N_SEG - 1], (xs_ref.shape[0] - padhi_ref[N_SEG - 1]) // zrows, zrows)

    pltpu.make_async_copy(h_ref, xs_ref.at[pl.ds(0, tm), :], sem).wait()


def _slot_spec(tm, index):
    return pl.BlockSpec((1, tm), lambda i, *_: (0, index(i)), memory_space=pltpu.SMEM)


def _dispatch(hp, slot, pad_lo, pad_hi, cap):
    n, dp = hp.shape
    tm = _tile(n, P_TM)
    grid_spec = pltpu.PrefetchScalarGridSpec(
        num_scalar_prefetch=2,
        grid=(n // tm,),
        in_specs=[_slot_spec(tm, lambda i: i), pl.BlockSpec((tm, dp), lambda i, *_: (i, 0))],
        out_specs=pl.BlockSpec(memory_space=pl.ANY),
        scratch_shapes=[pltpu.VMEM((P_ZROWS, dp), hp.dtype), pltpu.SemaphoreType.DMA(()),
                        pltpu.SemaphoreType.DMA(())],
    )
    return pl.pallas_call(
        _dispatch_kernel,
        grid_spec=grid_spec,
        out_shape=jax.ShapeDtypeStruct((cap, dp), hp.dtype),
        compiler_params=_cparams(("arbitrary",)),
        name="dispatch",
    )(pad_lo, pad_hi, slot, hp)


def _expert_kernel(ea_ref, eb_ref, nu_ref, xs_ref, w1a_ref, w3a_ref, w2a_ref,
                   w1b_ref, w3b_ref, w2b_ref, ys_ref):
    half = ys_ref.shape[1]

    @pl.when(pl.program_id(0) < nu_ref[0])
    def _():
        lo, hi = _unpack_pairs_f32(xs_ref[:, :half])
        x = jnp.concatenate([lo.astype(BF16), hi.astype(BF16)], axis=1)
        gate = lax.bitcast_convert_type(xs_ref[:, half:], F32)

        def ffn(w1_ref, w3_ref, w2_ref):
            a = jnp.dot(x, w1_ref[0], preferred_element_type=F32)
            b = jnp.dot(x, w3_ref[0], preferred_element_type=F32)
            h = (a * jax.nn.sigmoid(a)) * b
            return jnp.dot(h.astype(BF16), w2_ref[0], preferred_element_type=F32)

        y = gate[:, 0:1] * ffn(w1a_ref, w3a_ref, w2a_ref)
        y = y + gate[:, 1:2] * ffn(w1b_ref, w3b_ref, w2b_ref)
        ys_ref[...] = _pack_bf16_pairs(y.astype(BF16))

    @pl.when(pl.program_id(0) >= nu_ref[0])
    def _():
        ys_ref[...] = jnp.zeros_like(ys_ref)


def _experts(xs, block_ea, block_eb, n_used, w1, w3, w2):
    cap, dp = xs.shape
    n_blocks = cap // MOE_BLOCK
    d, de = w1.shape[1:]
    used = lambda i, nu: jnp.minimum(i, nu[0] - 1)
    blk = lambda i, ea, eb, nu: (used(i, nu), 0)
    wa = lambda i, ea, eb, nu: (ea[used(i, nu)], 0, 0)
    wb = lambda i, ea, eb, nu: (eb[used(i, nu)], 0, 0)
    grid_spec = pltpu.PrefetchScalarGridSpec(
        num_scalar_prefetch=3,
        grid=(n_blocks,),
        in_specs=[
            pl.BlockSpec((MOE_BLOCK, dp), blk),
            pl.BlockSpec((1, d, de), wa),
            pl.BlockSpec((1, d, de), wa),
            pl.BlockSpec((1, de, d), wa),
            pl.BlockSpec((1, d, de), wb),
            pl.BlockSpec((1, d, de), wb),
            pl.BlockSpec((1, de, d), wb),
        ],
        out_specs=pl.BlockSpec((MOE_BLOCK, d // 2), lambda i, ea, eb, nu: (i, 0)),
    )
    return pl.pallas_call(
        _expert_kernel,
        grid_spec=grid_spec,
        out_shape=jax.ShapeDtypeStruct((cap, d // 2), U32),
        compiler_params=_cparams(("arbitrary",)),
        name="experts",
    )(block_ea, block_eb, n_used, xs, w1, w3, w2, w1, w3, w2)


C_TM = 512


def _combine_kernel(slot_ref, slotn_ref, x_ref, mod_ref, modn_ref, gn_ref, ys_ref,
                    *rest, final):
    if final:
        o_ref, buf_ref, sem = rest
    else:
        o_ref, hn_ref, buf_ref, sem = rest
    tm = x_ref.shape[0]
    step = pl.program_id(0)
    last = pl.num_programs(0) - 1
    cur = step % 2
    nxt = 1 - cur

    def request(sl_ref, b):
        def body(i, carry):
            for j in range(ISSUE_UNROLL):
                r = i * ISSUE_UNROLL + j
                _row_copy(ys_ref, sl_ref[0, r], buf_ref.at[b], r, sem.at[b]).start(priority=j % 2)
            return carry

        lax.fori_loop(0, tm // ISSUE_UNROLL, body, 0)

    def wait_buffer(b):
        pltpu.make_async_copy(ys_ref.at[pl.ds(0, tm), :], buf_ref.at[b], sem.at[b]).wait()

    @pl.when(step == 0)
    def _():
        request(slot_ref, 0)

    request(slotn_ref, nxt)
    wait_buffer(cur)

    g2 = mod_ref[0, 5:6, :]
    gn = gn_ref[...]
    shift = modn_ref[0, 0:1, :]
    scale = modn_ref[0, 1:2, :]

    def rows_body(i, carry):
        rows = pl.ds(pl.multiple_of(i * NORM_ROWS, NORM_ROWS), NORM_ROWS)
        y = jnp.concatenate(_unpack_pairs_f32(buf_ref[cur, rows, :]), axis=1)
        out = x_ref[rows, :] + g2 * y
        if final:
            ms = jnp.mean(out * out, axis=-1, keepdims=True)
            o_ref[rows, :] = out * lax.rsqrt(ms + EPS) * gn
        else:
            o_ref[rows, :] = out
            hn_ref[rows, :] = _norm_mod_rows(out, gn, shift, scale).astype(hn_ref.dtype)
        return carry

    lax.fori_loop(0, tm // NORM_ROWS, rows_body, 0, unroll=NORM_UNROLL)

    @pl.when(step == last)
    def _():
        wait_buffer(nxt)


def _combine(x1, ys, slot, mod3, modn3, gn, seq, final):
    n, d = x1.shape
    tm = _tile(seq, C_TM)
    per_batch = seq // tm
    row = pl.BlockSpec((tm, d), lambda i, *_: (i, 0))
    mod_spec = pl.BlockSpec((1, N_ADA, d), lambda i, *_: (i // per_batch, 0, 0))
    grid_spec = pltpu.PrefetchScalarGridSpec(
        num_scalar_prefetch=0,
        grid=(n // tm,),
        in_specs=[
            _slot_spec(tm, lambda i: i),
            _slot_spec(tm, lambda i: jnp.minimum(i + 1, n // tm - 1)),
            row,
            mod_spec, mod_spec,
            pl.BlockSpec((1, d), lambda i, *_: (0, 0)),
            pl.BlockSpec(memory_space=pl.ANY),
        ],
        out_specs=row if final else [row, row],
        scratch_shapes=[pltpu.VMEM((2, tm, ys.shape[1]), ys.dtype),
                        pltpu.SemaphoreType.DMA((2,))],
    )
    x_sds = jax.ShapeDtypeStruct((n, d), F32)
    return pl.pallas_call(
        functools.partial(_combine_kernel, final=final),
        grid_spec=grid_spec,
        out_shape=x_sds if final else [x_sds, jax.ShapeDtypeStruct((n, d), BF16)],
        compiler_params=_cparams(("arbitrary",)),
        name="combine",
    )(slot, slot, x1, mod3, modn3, gn, ys)


def _lookup(table, idx):
    table = jnp.asarray(table, I32)
    hit = idx[..., None] == jnp.arange(table.shape[0], dtype=I32)
    return jnp.sum(jnp.where(hit, table, 0), axis=-1).astype(I32)


def _moe(x1, mod3, g2, modn3, gn, wr, router_b, w1, w3, w2, seq, final):
    n, d = x1.shape
    hp, seg_pos, cnt = _route(x1, mod3, g2, wr, router_b, seq)
    counts = cnt[:N_SEG, 0].astype(I32)
    padded = (counts + MOE_BLOCK - 1) // MOE_BLOCK * MOE_BLOCK
    pends = jnp.cumsum(padded).astype(I32)
    pstarts = pends - padded
    n_blocks = -(-n // MOE_BLOCK) + N_SEG
    block_lo = jnp.arange(n_blocks, dtype=I32) * MOE_BLOCK
    block_seg = jnp.minimum(jnp.sum((pends[None, :] <= block_lo[:, None]).astype(I32), axis=1),
                            N_SEG - 1).astype(I32)
    n_used = pends[-1:] // MOE_BLOCK
    slot = (_lookup(pstarts, seg_pos[0]) + seg_pos[1]).reshape(1, n)
    xs = _dispatch(hp, slot, pstarts + counts, pends, n_blocks * MOE_BLOCK)
    ys = _experts(xs, _lookup(SEG_A, block_seg), _lookup(SEG_B, block_seg), n_used, w1, w3, w2)
    return _combine(x1, ys, slot, mod3, modn3, gn, seq, final)


def _fft_split(seq):
    n1_len = 1 << (int(math.log2(seq)) // 2)
    return n1_len, seq // n1_len


def kernel(x, c, positions, ada_w, ada_b, norm1_g, norm2_g, w_in, w_fourier_out, w_attn_out,
           attn_sink, w_out, router_w, router_b, expert_w1, expert_w3, expert_w2, final_norm_g):
    n_batch, seq, d = x.shape
    depth = ada_w.shape[0]
    n1_len, n2_len = _fft_split(seq)
    assert n1_len * n2_len == seq and n1_len % F_SUB == 0 and n2_len % F_SUB == 0
    assert seq % ATT_Q == 0 and seq >= ATT_KW and d == D_MODEL

    mod = _ada_mod(c, ada_w, ada_b).reshape(depth, n_batch, N_ADA, d)
    tabs = _rope_tables(positions)
    consts = _dft_constants(n1_len, n2_len)
    w_in_bf = _reorder_cast_w_in(w_in)
    wr = jnp.pad(router_w, ((0, 0), (0, ROUTER_PAD - N_EXPERTS))).astype(BF16)
    wfo, wao, wout = (w.astype(BF16) for w in (w_fourier_out, w_attn_out, w_out))

    x2 = x.reshape(n_batch * seq, d)
    h = _entry_norm(x2, mod[0], norm1_g[0].reshape(1, d), seq)
    for l in range(depth):
        final = l == depth - 1
        u, (ew1, ew3, ew2) = _in_proj(h, w_in_bf, l, (expert_w1, expert_w3, expert_w2))
        yf = _fourier(u, n_batch, seq, consts)
        q_rot, k_rot = _rope(u, tabs)
        att = _attention(q_rot, k_rot, u, attn_sink[l], n_batch, seq)
        x1 = _mix_out(yf, att, u, x2, mod[l], wfo, wao, wout, l, seq)
        gn = (final_norm_g if final else norm1_g[l + 1]).reshape(1, d)
        res = _moe(x1, mod[l], norm2_g[l].reshape(1, d), mod[l if final else l + 1], gn, wr,
                   router_b, ew1, ew3, ew2, seq, final)
        if final:
            x2 = res
        else:
            x2, h = res
    return x2.reshape(n_batch, seq, d)
```

```python
import functools
import math

import numpy as np
import jax
import jax.numpy as jnp
from jax import lax
from jax.experimental import pallas as pl
from jax.experimental.pallas import tpu as pltpu

F32 = jnp.float32
BF16 = jnp.bfloat16
I32 = jnp.int32
U32 = jnp.uint32

D_MODEL = 2048
N_ADA = 6
F_GROUPS = 4
F_GDIM = 256
F_WIDTH = F_GROUPS * F_GDIM
N_HEADS = 16
N_KV = 2
HEAD_DIM = 64
HEADS_PER_KV = N_HEADS // N_KV
Q_WIDTH = N_HEADS * HEAD_DIM
KV_WIDTH = N_KV * HEAD_DIM
ROPE_DIM = HEAD_DIM // 4
ROPE_HALF = ROPE_DIM // 2
ROPE_THETA = 500000.0
WINDOW = 128
N_EXPERTS = 16
N_GROUPS = 4
EPG = N_EXPERTS // N_GROUPS
TOP_K = 2
D_EXPERT = 768
MOE_BLOCK = 512
EPS = 1e-6
NEG_INF = -1e30
IN_WIDTH = F_WIDTH + Q_WIDTH + 2 * KV_WIDTH + 2 * D_MODEL

COL_GF = (F_WIDTH + Q_WIDTH) // D_MODEL
COL_GA = COL_GF + 1
COL_K = (F_WIDTH + Q_WIDTH + 2 * D_MODEL) // KV_WIDTH
COL_V = COL_K + 1

LANES = 128
SUBLANES = 8
BF16_ROWS = 16
VMEM_LIMIT = 56 * 1024 * 1024

ATT_Q = 256
ATT_KW = ATT_Q + 2 * WINDOW
ATT_SUB = 128
ATT_SW = ATT_SUB + 2 * WINDOW
ATT_GROUP = 4


def _cparams(sem, vmem=VMEM_LIMIT):
    return pltpu.CompilerParams(dimension_semantics=sem, vmem_limit_bytes=vmem)


def _tile(n, pref):
    t = min(n, pref)
    while n % t:
        t //= 2
    return t


ADA_TN = 768
ADA_ROWS = 32


def _ada_kernel(cb_ref, w_ref, b_ref, o_ref, *, n_batch):
    tn = w_ref.shape[-1]
    n_chunks = w_ref.shape[1] // ADA_ROWS

    def body(i, accs):
        r = pl.multiple_of(i * ADA_ROWS, ADA_ROWS)
        w = w_ref[0, pl.ds(r, ADA_ROWS), :].reshape(ADA_ROWS // SUBLANES, SUBLANES, tn)
        out = []
        for b in range(n_batch):
            c = cb_ref[b, pl.ds(r, ADA_ROWS), :]
            ca = c * jax.nn.sigmoid(c)
            cr = jnp.tile(ca, (1, tn // LANES)).reshape(ADA_ROWS // SUBLANES, SUBLANES, tn)
            out.append(accs[b] + jnp.sum(w * cr, axis=0))
        return tuple(out)

    accs = lax.fori_loop(0, n_chunks, body,
                         tuple(jnp.zeros((SUBLANES, tn), F32) for _ in range(n_batch)))
    for b in range(n_batch):
        o_ref[0, b:b + 1, :] = jnp.sum(accs[b], axis=0, keepdims=True) + b_ref[0]


def _ada_mod(c, ada_w, ada_b):
    n_layers, d, width = ada_w.shape
    n_batch = c.shape[0]
    cb = jnp.broadcast_to(c[:, :, None], (n_batch, d, LANES))
    return pl.pallas_call(
        functools.partial(_ada_kernel, n_batch=n_batch),
        grid=(n_layers, width // ADA_TN),
        in_specs=[
            pl.BlockSpec((n_batch, d, LANES), lambda l, j: (0, 0, 0)),
            pl.BlockSpec((1, d, ADA_TN), lambda l, j: (l, 0, j)),
            pl.BlockSpec((1, 1, ADA_TN), lambda l, j: (l, 0, j)),
        ],
        out_specs=pl.BlockSpec((1, n_batch, ADA_TN), lambda l, j: (l, 0, j)),
        out_shape=jax.ShapeDtypeStruct((n_layers, n_batch, width), F32),
        compiler_params=_cparams(("arbitrary", "arbitrary")),
        name="ada_mod",
    )(cb, ada_w, ada_b.reshape(n_layers, 1, width))


NORM_ROWS = 16
NORM_UNROLL = 8


def _norm_mod_rows(x, g, shift, scale):
    ms = jnp.mean(x * x, axis=-1, keepdims=True)
    y = x * lax.rsqrt(ms + EPS) * g
    return y * (1.0 + scale) + shift


A_TM = 1024
A_TN = 1280


def _rope_chunks(tn):
    cols = [(c, True) for c in range(F_WIDTH, F_WIDTH + Q_WIDTH, LANES)]
    cols.append((COL_K * KV_WIDTH, False))
    by_tile = {}
    for c, is_q in cols:
        by_tile.setdefault(c // tn, []).append((c % tn, is_q))
    return by_tile


def _a_kernel(h_ref, w_ref, kc_ref, ksa_ref, ksb_ref, qc_ref, qsa_ref, qsb_ref,
              c1_ref, c3_ref, c2_ref, o_ref, b1_ref, b3_ref, b2_ref):
    o_ref[...] = jnp.dot(h_ref[...], w_ref[...], preferred_element_type=F32).astype(o_ref.dtype)
    for tile, chunks in _rope_chunks(o_ref.shape[1]).items():
        @pl.when(pl.program_id(1) == tile)
        def _():
            for col, is_q in chunks:
                c_ref, sa_ref, sb_ref = (qc_ref, qsa_ref, qsb_ref) if is_q else (kc_ref, ksa_ref, ksb_ref)
                t = o_ref[:, col:col + LANES].astype(F32)
                r = (t * c_ref[...] + pltpu.roll(t, LANES - ROPE_HALF, 1) * sa_ref[...]
                     + pltpu.roll(t, ROPE_HALF, 1) * sb_ref[...])
                o_ref[:, col:col + LANES] = r.astype(o_ref.dtype)
    for src, dst in ((c1_ref, b1_ref), (c3_ref, b3_ref), (c2_ref, b2_ref)):
        dst[...] = src[...].astype(dst.dtype)


def _in_proj(h, w_bf, layer, tabs, expert_ws):
    n, d = h.shape
    width = w_bf.shape[-1]
    tm = _tile(n, A_TM)
    tn = _tile(width, A_TN)
    tab_spec = pl.BlockSpec((tm, LANES), lambda i, j: (i, 0))
    nj = width // tn
    steps = (n // tm) * nj
    n_chunks = 1 << (steps.bit_length() - 1)
    chunk_id = lambda i, j: jnp.minimum(i * nj + j, n_chunks - 1)
    side_in, side_out, side_shapes, flats = [], [], [], []
    for w in expert_ws:
        depth, n_exp, rows, cols = w.shape
        per_layer = n_exp * rows
        chunk = per_layer // n_chunks
        assert per_layer % n_chunks == 0 and chunk % BF16_ROWS == 0
        flats.append(w.reshape(depth * per_layer, cols))
        side_in.append(
            pl.BlockSpec((chunk, cols), lambda i, j: (layer * n_chunks + chunk_id(i, j), 0)))
        side_out.append(pl.BlockSpec((chunk, cols), lambda i, j: (chunk_id(i, j), 0)))
        side_shapes.append(jax.ShapeDtypeStruct((per_layer, cols), BF16))
    outs = pl.pallas_call(
        _a_kernel,
        grid=(n // tm, nj),
        in_specs=[
            pl.BlockSpec((tm, d), lambda i, j: (i, 0)),
            pl.BlockSpec((None, d, tn), lambda i, j: (layer, 0, j)),
        ] + [tab_spec] * len(tabs) + side_in,
        out_specs=[pl.BlockSpec((tm, tn), lambda i, j: (i, j))] + side_out,
        out_shape=[jax.ShapeDtypeStruct((n, width), BF16)] + side_shapes,
        compiler_params=_cparams(("arbitrary", "arbitrary")),
        name="in_proj",
    )(h, w_bf, *tabs, *flats)
    casts = [o.reshape(w.shape[1:]) for o, w in zip(outs[1:], expert_ws)]
    return outs[0], casts


W_TILE = 2 * KV_WIDTH


def _cast_kernel(w_ref, o_ref):
    o_ref[...] = w_ref[...].astype(o_ref.dtype)


def _reorder_cast_w_in(w_in):
    depth, d, width = w_in.shape
    kv_tile = (F_WIDTH + Q_WIDTH) // W_TILE
    n_tiles = width // W_TILE
    assert (F_WIDTH + Q_WIDTH) % W_TILE == 0 and width % W_TILE == 0

    def src_tile(j):
        return jnp.where(j < kv_tile, j, jnp.where(j < n_tiles - 1, j + 1, kv_tile))

    return pl.pallas_call(
        _cast_kernel,
        grid=(depth, n_tiles),
        in_specs=[pl.BlockSpec((1, d, W_TILE), lambda l, j: (l, 0, src_tile(j)))],
        out_specs=pl.BlockSpec((1, d, W_TILE), lambda l, j: (l, 0, j)),
        out_shape=jax.ShapeDtypeStruct((depth, d, width), BF16),
        compiler_params=_cparams(("arbitrary", "arbitrary")),
        name="w_in_cast",
    )(w_in)


N_TM = 512


def _norm_rows_to(x_ref, dst_ref, g, shift, scale):
    def body(i, carry):
        r = pl.multiple_of(i * NORM_ROWS, NORM_ROWS)
        h = _norm_mod_rows(x_ref[pl.ds(r, NORM_ROWS), :], g, shift, scale)
        dst_ref[pl.ds(r, NORM_ROWS), :] = h.astype(dst_ref.dtype)
        return carry

    lax.fori_loop(0, x_ref.shape[0] // NORM_ROWS, body, 0, unroll=NORM_UNROLL)


def _norm_kernel(x_ref, mod_ref, g_ref, h_ref):
    _norm_rows_to(x_ref, h_ref, g_ref[...], mod_ref[0, 0:1, :], mod_ref[0, 1:2, :])


def _entry_norm(x2, mod3, g, seq):
    n, d = x2.shape
    tm = _tile(seq, N_TM)
    per_batch = seq // tm
    return pl.pallas_call(
        _norm_kernel,
        grid=(n // tm,),
        in_specs=[
            pl.BlockSpec((tm, d), lambda i: (i, 0)),
            pl.BlockSpec((1, N_ADA, d), lambda i: (i // per_batch, 0, 0)),
            pl.BlockSpec((1, d), lambda i: (0, 0)),
        ],
        out_specs=pl.BlockSpec((tm, d), lambda i: (i, 0)),
        out_shape=jax.ShapeDtypeStruct((n, d), BF16),
        compiler_params=_cparams(("arbitrary",)),
        name="entry_norm",
    )(x2, mod3, g)


LOG2E = math.log2(math.e)
Q_SCALE = HEAD_DIM ** -0.5 * LOG2E


def _rope_tables(positions):
    inv = ROPE_THETA ** (-jnp.arange(0, ROPE_DIM, 2, dtype=F32) / ROPE_DIM)
    ang = positions.astype(F32)[..., None] * inv
    cos = jnp.cos(ang).reshape(-1, ROPE_HALF)
    sin = jnp.sin(ang).reshape(-1, ROPE_HALF)
    n = cos.shape[0]
    rest = HEAD_DIM - ROPE_DIM
    ones = jnp.ones((n, rest), F32)
    z_half = jnp.zeros((n, ROPE_HALF), F32)
    z_rest = jnp.zeros((n, rest), F32)
    reps = LANES // HEAD_DIM
    c = jnp.tile(jnp.concatenate([cos, cos, ones], axis=1), (1, reps))
    sa = jnp.tile(jnp.concatenate([-sin, z_half, z_rest], axis=1), (1, reps))
    sb = jnp.tile(jnp.concatenate([z_half, sin, z_rest], axis=1), (1, reps))
    return c, sa, sb, c * Q_SCALE, sa * Q_SCALE, sb * Q_SCALE


F_SUB = BF16_ROWS


def _dft_constants(n1_len, n2_len):
    cidx = np.arange(F_GDIM)
    ang = 2.0 * np.pi * np.outer(cidx, cidx) / F_GDIM
    wc = np.concatenate([np.cos(ang), -np.sin(ang)], axis=1) / math.sqrt(F_GDIM)
    a = np.arange(n1_len)
    ang1 = 2.0 * np.pi * np.outer(a, a) / n1_len
    fr, fi = np.cos(ang1), -np.sin(ang1)
    fbig = np.block([[fr, -fi], [fi, fr]]) / math.sqrt(n1_len)
    k1 = np.arange(n1_len)[:, None, None]
    k2 = np.arange(n2_len)[None, :, None]
    b = np.arange(n2_len)[None, None, :]
    ang2 = 2.0 * np.pi * (b * k2 / n2_len + b * k1 / (n1_len * n2_len))
    g = np.concatenate([np.cos(ang2), np.sin(ang2)], axis=2) / math.sqrt(n2_len)
    return (jnp.asarray(wc, BF16), jnp.asarray(fbig, BF16), jnp.asarray(g, BF16))


def _f12_kernel(u_ref, wc_ref, fb_ref, t_ref):
    n1_len = u_ref.shape[1]
    x = u_ref[0].reshape(n1_len * F_SUB, F_WIDTH)
    zr, zi = [], []
    for g in range(F_GROUPS):
        r = jnp.dot(x[:, g * F_GDIM:(g + 1) * F_GDIM], wc_ref[...], preferred_element_type=F32)
        zr.append(r[:, :F_GDIM].astype(BF16))
        zi.append(r[:, F_GDIM:].astype(BF16))
    zr = jnp.concatenate(zr, axis=1).reshape(n1_len, F_SUB, F_WIDTH)
    zi = jnp.concatenate(zi, axis=1).reshape(n1_len, F_SUB, F_WIDTH)
    zr = pltpu.einshape("mnc->nmc", zr)
    zi = pltpu.einshape("mnc->nmc", zi)
    for j in range(F_SUB):
        rhs = jnp.concatenate([zr[j], zi[j]], axis=0)
        t = jnp.dot(fb_ref[...], rhs, preferred_element_type=F32)
        t_ref[0, j] = t.astype(BF16).reshape(2, n1_len, F_WIDTH)


def _f3_kernel(t_ref, g_ref, y_ref):
    tr = pltpu.einshape("nkc->knc", t_ref[0, :, 0])
    ti = pltpu.einshape("nkc->knc", t_ref[0, :, 1])
    ys = []
    for j in range(F_SUB):
        rhs = jnp.concatenate([tr[j], ti[j]], axis=0)
        ys.append(jnp.dot(g_ref[j], rhs, preferred_element_type=F32).astype(BF16))
    y_ref[0] = pltpu.einshape("knc->nkc", jnp.stack(ys, axis=0))


def _fourier(u, n_batch, seq, consts):
    wc, fbig, gmat = consts
    n1_len = fbig.shape[0] // 2
    n2_len = seq // n1_len
    u5 = u.reshape(n_batch, n1_len, n2_len, u.shape[-1])
    t = pl.pallas_call(
        _f12_kernel,
        grid=(n_batch, n2_len // F_SUB),
        in_specs=[
            pl.BlockSpec((1, n1_len, F_SUB, F_WIDTH), lambda b, j: (b, 0, j, 0)),
            pl.BlockSpec(wc.shape, lambda b, j: (0, 0)),
            pl.BlockSpec(fbig.shape, lambda b, j: (0, 0)),
        ],
        out_specs=pl.BlockSpec((1, F_SUB, 2, n1_len, F_WIDTH), lambda b, j: (b, j, 0, 0, 0)),
        out_shape=jax.ShapeDtypeStruct((n_batch, n2_len, 2, n1_len, F_WIDTH), BF16),
        compiler_params=_cparams(("arbitrary", "arbitrary")),
        name="fourier_stage1",
    )(u5, wc, fbig)
    y = pl.pallas_call(
        _f3_kernel,
        grid=(n_batch, n1_len // F_SUB),
        in_specs=[
            pl.BlockSpec((1, n2_len, 2, F_SUB, F_WIDTH), lambda b, j: (b, 0, 0, j, 0)),
            pl.BlockSpec((F_SUB, n2_len, 2 * n2_len), lambda b, j: (j, 0, 0)),
        ],
        out_specs=pl.BlockSpec((1, n2_len, F_SUB, F_WIDTH), lambda b, j: (b, 0, j, 0)),
        out_shape=jax.ShapeDtypeStruct((n_batch, n2_len, n1_len, F_WIDTH), BF16),
        compiler_params=_cparams(("arbitrary", "arbitrary")),
        name="fourier_stage2",
    )(t, gmat)
    return y.reshape(n_batch * seq, F_WIDTH)


def _att_kernel(sink_ref, q_ref, k_ref, v_ref, o_ref, kv_ref, *, seq):
    i = pl.program_id(1)
    s0 = i * ATT_Q
    ws = pl.multiple_of(jnp.clip(s0 - WINDOW, 0, seq - ATT_KW), WINDOW)
    off = s0 - ws
    lane = lax.broadcasted_iota(I32, (ATT_KW, LANES), 1)
    low = lane < HEAD_DIM
    for t, ref in enumerate((k_ref, v_ref)):
        w = ref[0, pl.ds(ws, ATT_KW), :].astype(F32)
        w_sw = pltpu.roll(w, HEAD_DIM, 1)
        for h in range(N_KV):
            own, oth = (w, w_sw) if h == 0 else (w_sw, w)
            kv_ref[t, h, 0] = jnp.where(low, own, 0.0).astype(BF16)
            kv_ref[t, h, 1] = jnp.where(low, 0.0, oth).astype(BF16)

    def sub_block(a, carry):
        r0 = pl.multiple_of(a * ATT_SUB, ATT_SUB)
        c0 = pl.multiple_of(
            jnp.clip(((off + r0) // WINDOW) * WINDOW - WINDOW, 0, ATT_KW - ATT_SW), WINDOW)
        row = lax.broadcasted_iota(I32, (ATT_SUB, ATT_SW), 0)
        col = lax.broadcasted_iota(I32, (ATT_SUB, ATT_SW), 1)
        rel = (c0 + col) - (off + r0 + row)
        bias = jnp.where(jnp.abs(rel) <= WINDOW, 0.0, NEG_INF).astype(F32)
        pairs = [(h, m) for h in range(N_KV) for m in range(HEADS_PER_KV // 2)]
        for g0 in range(0, len(pairs), ATT_GROUP):
            group = [(h, m, half) for (h, m) in pairs[g0:g0 + ATT_GROUP] for half in range(2)]
            scores = []
            for h, m, half in group:
                cb = (h * HEADS_PER_KV // 2 + m) * LANES
                q2 = q_ref[pl.ds(r0, ATT_SUB), cb:cb + LANES]
                kk = kv_ref[0, h, half, pl.ds(c0, ATT_SW), :]
                scores.append(lax.dot_general(q2, kk, (((1,), (1,)), ((), ())),
                                              preferred_element_type=F32))
            probs, rdens = [], []
            for (h, m, half), s in zip(group, scores):
                sink = sink_ref[h * HEADS_PER_KV + 2 * m + half] * LOG2E
                s = s + bias
                mx = jnp.maximum(jnp.max(s, axis=-1, keepdims=True), sink)
                p = jnp.exp2(s - mx)
                rdens.append(1.0 / (jnp.sum(p, axis=-1, keepdims=True) + jnp.exp2(sink - mx)))
                probs.append(p.astype(BF16))
            outs = []
            for (h, m, half), p, rden in zip(group, probs, rdens):
                vv = kv_ref[1, h, half, pl.ds(c0, ATT_SW), :]
                outs.append(jnp.dot(p, vv, preferred_element_type=F32) * rden)
            for j, (h, m) in enumerate(pairs[g0:g0 + ATT_GROUP]):
                cb = (h * HEADS_PER_KV // 2 + m) * LANES
                o_ref[pl.ds(r0, ATT_SUB), cb:cb + LANES] = (outs[2 * j] + outs[2 * j + 1]).astype(BF16)
        return carry

    lax.fori_loop(0, ATT_Q // ATT_SUB, sub_block, 0)


def _attention(u, sink, n_batch, seq):
    n = u.shape[0]
    nq = seq // ATT_Q
    u3 = u.reshape(n_batch, seq, u.shape[-1])
    return pl.pallas_call(
        functools.partial(_att_kernel, seq=seq),
        grid=(n_batch, nq),
        in_specs=[
            pl.BlockSpec(memory_space=pltpu.SMEM),
            pl.BlockSpec((ATT_Q, Q_WIDTH), lambda b, i: (b * nq + i, F_WIDTH // Q_WIDTH)),
            pl.BlockSpec((1, seq, KV_WIDTH), lambda b, i: (b, 0, COL_K)),
            pl.BlockSpec((1, seq, KV_WIDTH), lambda b, i: (b, 0, COL_V)),
        ],
        out_specs=pl.BlockSpec((ATT_Q, Q_WIDTH), lambda b, i: (b * nq + i, 0)),
        out_shape=jax.ShapeDtypeStruct((n, Q_WIDTH), BF16),
        scratch_shapes=[pltpu.VMEM((2, N_KV, 2, ATT_KW, LANES), BF16)],
        compiler_params=_cparams(("arbitrary", "arbitrary")),
        name="window_attention",
    )(sink, u, u3, u3)


D_TM = 512
ROUTER_PAD = LANES


def _d_kernel(f_ref, a_ref, gf_ref, ga_ref, x_ref, mod_ref, wfo_ref, wao_ref, wout_ref, x1_ref):
    yf = jnp.dot(f_ref[...], wfo_ref[...], preferred_element_type=F32)
    ya = jnp.dot(a_ref[...], wao_ref[...], preferred_element_type=F32)
    merged = (jax.nn.sigmoid(gf_ref[...].astype(F32)) * yf
              + jax.nn.sigmoid(ga_ref[...].astype(F32)) * ya)
    out = jnp.dot(merged.astype(BF16), wout_ref[...], preferred_element_type=F32)
    x1_ref[...] = x_ref[...] + mod_ref[0, 2:3, :] * out


def _layer_weight_spec(w, layer):
    return pl.BlockSpec((None,) + w.shape[1:], lambda i: (layer, 0, 0),
                        pipeline_mode=pl.Buffered(1))


def _mix_out(yf, att, u, x2, mod3, wfo, wao, wout, layer, seq):
    n, d = x2.shape
    tm = _tile(seq, D_TM)
    per_batch = seq // tm
    row = lambda w: pl.BlockSpec((tm, w), lambda i: (i, 0))
    return pl.pallas_call(
        _d_kernel,
        grid=(n // tm,),
        in_specs=[
            row(F_WIDTH), row(Q_WIDTH),
            pl.BlockSpec((tm, d), lambda i: (i, COL_GF)),
            pl.BlockSpec((tm, d), lambda i: (i, COL_GA)),
            row(d),
            pl.BlockSpec((1, N_ADA, d), lambda i: (i // per_batch, 0, 0)),
            _layer_weight_spec(wfo, layer), _layer_weight_spec(wao, layer),
            _layer_weight_spec(wout, layer),
        ],
        out_specs=row(d),
        out_shape=jax.ShapeDtypeStruct((n, d), F32),
        compiler_params=_cparams(("arbitrary",)),
        name="mix_out",
    )(yf, att, u, u, x2, mod3, wfo, wao, wout)


R_TN = 1024
R_SUB = 512
PAIRS = [(a, b) for a in range(EPG) for b in range(a + 1, EPG)]
N_SEG = N_GROUPS * len(PAIRS)
SEG_PAD = 32
SEG_A = [g * EPG + a for g in range(N_GROUPS) for a, _ in PAIRS]
SEG_B = [g * EPG + b for g in range(N_GROUPS) for _, b in PAIRS]
HI_MASK = 0xFFFF0000


def _pack_bf16_pairs(hb):
    bits = lax.bitcast_convert_type(hb.astype(F32), U32)
    half = hb.shape[1] // 2
    return (bits[:, half:] & U32(HI_MASK)) | (bits[:, :half] >> U32(16))


def _unpack_pairs_f32(xu):
    return (lax.bitcast_convert_type(xu << U32(16), F32),
            lax.bitcast_convert_type(xu & U32(HI_MASK), F32))


def _route_kernel(bias_ref, x_ref, mod_ref, g2_ref, wr_ref, tri_ref,
                  hp_ref, sp_ref, cnt_ref, carry_ref, hb_ref):
    @pl.when(pl.program_id(0) == 0)
    def _():
        carry_ref[...] = jnp.zeros_like(carry_ref)

    tn, d = x_ref.shape
    half = d // 2
    g2 = g2_ref[...]
    shift = mod_ref[0, 3:4, :]
    scale = mod_ref[0, 4:5, :]

    def norm_body(i, c):
        r = pl.multiple_of(i * NORM_ROWS, NORM_ROWS)
        hb = _norm_mod_rows(x_ref[pl.ds(r, NORM_ROWS), :], g2, shift, scale).astype(BF16)
        hb_ref[pl.ds(r, NORM_ROWS), :] = hb
        hp_ref[pl.ds(r, NORM_ROWS), :half] = _pack_bf16_pairs(hb)
        return c

    lax.fori_loop(0, tn // NORM_ROWS, norm_body, 0, unroll=NORM_UNROLL)
    logits = jnp.dot(hb_ref[...], wr_ref[...], preferred_element_type=F32)
    lt = logits.T
    sc = [jax.nn.sigmoid(lt[e:e + 1, :]) for e in range(N_EXPERTS)]
    bz = [sc[e] + bias_ref[e] for e in range(N_EXPERTS)]
    one = jnp.ones((1, tn), F32)
    zero = jnp.zeros((1, tn), F32)
    in_top, gscore = [], []
    for g in range(N_GROUPS):
        v = bz[g * EPG:(g + 1) * EPG]
        rank = [zero] * EPG
        for a in range(EPG):
            for b in range(a + 1, EPG):
                a_wins = jnp.where(v[a] >= v[b], one, zero)
                rank[b] = rank[b] + a_wins
                rank[a] = rank[a] + (one - a_wins)
        top = [jnp.where(rank[a] < TOP_K - 0.5, one, zero) for a in range(EPG)]
        in_top += top
        gs = zero
        for a in range(EPG):
            gs = gs + top[a] * v[a]
        gscore.append(gs)
    sel = []
    for g in range(N_GROUPS):
        ok = one
        for other in range(N_GROUPS):
            if other < g:
                ok = ok * jnp.where(gscore[g] > gscore[other], one, zero)
            elif other > g:
                ok = ok * jnp.where(gscore[g] >= gscore[other], one, zero)
        sel += [in_top[g * EPG + a] * ok for a in range(EPG)]

    segsel = [sel[g * EPG + a] * sel[g * EPG + b] for g in range(N_GROUPS) for a, b in PAIRS]
    selm = jnp.concatenate(segsel + [zero] * (SEG_PAD - N_SEG), axis=0)
    carry = carry_ref[:, 0:1]
    cums = []
    for c in range(tn // R_SUB):
        blk = selm[:, c * R_SUB:(c + 1) * R_SUB]
        cums.append(jnp.dot(blk.astype(BF16), tri_ref[...], preferred_element_type=F32) + carry)
        carry = carry + jnp.sum(blk, axis=1, keepdims=True)
    cum = jnp.concatenate(cums, axis=1)
    carry_ref[...] = jnp.broadcast_to(carry, carry_ref.shape)
    cnt_ref[...] = jnp.broadcast_to(carry, cnt_ref.shape)

    seg = zero
    pos = zero
    for s_id in range(N_SEG):
        seg = seg + segsel[s_id] * float(s_id)
        pos = pos + segsel[s_id] * cum[s_id:s_id + 1, :]
    sp_ref[0:1, :] = seg.astype(I32)
    sp_ref[1:2, :] = pos.astype(I32)

    before = zero
    score = [zero, zero]
    for e in range(N_EXPERTS):
        score[0] = score[0] + sel[e] * jnp.where(before < 0.5, one, zero) * sc[e]
        score[1] = score[1] + sel[e] * jnp.where(before >= 0.5, one, zero) * sc[e]
        before = before + sel[e]
    tot = score[0] + score[1]
    rows = lax.broadcasted_iota(I32, (LANES, tn), 0)
    wmat = jnp.where(rows == 0, score[0] / tot, jnp.where(rows == 1, score[1] / tot, 0.0))
    hp_ref[:, half:] = lax.bitcast_convert_type(wmat.T, U32)


def _route(x1, mod3, g2, wr, router_b, seq):
    n, d = x1.shape
    tn = _tile(seq, R_TN)
    per_batch = seq // tn
    tri = jnp.asarray(np.triu(np.ones((R_SUB, R_SUB), np.float32), k=1), BF16)
    return pl.pallas_call(
        _route_kernel,
        grid=(n // tn,),
        in_specs=[
            pl.BlockSpec(memory_space=pltpu.SMEM),
            pl.BlockSpec((tn, d), lambda i: (i, 0)),
            pl.BlockSpec((1, N_ADA, d), lambda i: (i // per_batch, 0, 0)),
            pl.BlockSpec((1, d), lambda i: (0, 0)),
            pl.BlockSpec(wr.shape, lambda i: (0, 0)),
            pl.BlockSpec((R_SUB, R_SUB), lambda i: (0, 0)),
        ],
        out_specs=[pl.BlockSpec((tn, d // 2 + LANES), lambda i: (i, 0)),
                   pl.BlockSpec((2, tn), lambda i: (0, i)),
                   pl.BlockSpec((SEG_PAD, LANES), lambda i: (0, 0))],
        out_shape=[jax.ShapeDtypeStruct((n, d // 2 + LANES), U32),
                   jax.ShapeDtypeStruct((2, n), I32),
                   jax.ShapeDtypeStruct((SEG_PAD, LANES), F32)],
        scratch_shapes=[pltpu.VMEM((SEG_PAD, LANES), F32), pltpu.VMEM((tn, d), BF16)],
        compiler_params=_cparams(("arbitrary",)),
        name="route",
    )(router_b, x1, mod3, g2, wr, tri)


P_TM = 1024
P_ZROWS = 256
ISSUE_UNROLL = 8


def _row_copy(src, src_row, dst, dst_row, sem):
    return pltpu.make_async_copy(src.at[pl.ds(src_row, 1), :], dst.at[pl.ds(dst_row, 1), :], sem)


def _dispatch_kernel(padlo_ref, padhi_ref, slot_ref, h_ref, xs_ref, zrow_ref, sem, zsem):
    tm = h_ref.shape[0]

    def issue(i, carry):
        for j in range(ISSUE_UNROLL):
            r = i * ISSUE_UNROLL + j
            _row_copy(h_ref, r, xs_ref, slot_ref[0, r], sem).start(priority=j % 2)
        return carry

    lax.fori_loop(0, tm // ISSUE_UNROLL, issue, 0)

    @pl.when(pl.program_id(0) == pl.num_programs(0) - 1)
    def _():
        zrow_ref[...] = jnp.zeros_like(zrow_ref)
        zrows = zrow_ref.shape[0]

        def zissue(r, carry):
            _row_copy(zrow_ref, 0, xs_ref, r, zsem).start()
            return carry

        def zwait(r, carry):
            _row_copy(zrow_ref, 0, xs_ref, r, zsem).wait()
            return carry

        def chunked(first, count, rows):
            def chunk(j):
                start = pl.multiple_of(first + j * rows, rows)
                return pltpu.make_async_copy(zrow_ref.at[pl.ds(0, rows), :],
                                             xs_ref.at[pl.ds(start, rows), :], zsem)

            lax.fori_loop(0, count, lambda j, c: (chunk(j).start(), c)[1], 0)
            lax.fori_loop(0, count, lambda j, c: (chunk(j).wait(), c)[1], 0)

        for s_id in range(N_SEG):
            lo = padlo_ref[s_id]
            hi = padhi_ref[s_id]
            aligned = jnp.minimum(hi, (lo + SUBLANES - 1) // SUBLANES * SUBLANES)
            lax.fori_loop(lo, aligned, zissue, 0)
            lax.fori_loop(lo, aligned, zwait, 0)
            chunked(aligned, (hi - aligned) // SUBLANES, SUBLANES)

        chunked(padhi_ref[N_SEG - 1], (xs_ref.shape[0] - padhi_ref[N_SEG - 1]) // zrows, zrows)

    pltpu.make_async_copy(h_ref, xs_ref.at[pl.ds(0, tm), :], sem).wait()


def _slot_spec(tm, index):
    return pl.BlockSpec((1, tm), lambda i, *_: (0, index(i)), memory_space=pltpu.SMEM)


def _dispatch(hp, slot, pad_lo, pad_hi, cap):
    n, dp = hp.shape
    tm = _tile(n, P_TM)
    grid_spec = pltpu.PrefetchScalarGridSpec(
        num_scalar_prefetch=2,
        grid=(n // tm,),
        in_specs=[_slot_spec(tm, lambda i: i), pl.BlockSpec((tm, dp), lambda i, *_: (i, 0))],
        out_specs=pl.BlockSpec(memory_space=pl.ANY),
        scratch_shapes=[pltpu.VMEM((P_ZROWS, dp), hp.dtype), pltpu.SemaphoreType.DMA(()),
                        pltpu.SemaphoreType.DMA(())],
    )
    return pl.pallas_call(
        _dispatch_kernel,
        grid_spec=grid_spec,
        out_shape=jax.ShapeDtypeStruct((cap, dp), hp.dtype),
        compiler_params=_cparams(("arbitrary",)),
        name="dispatch",
    )(pad_lo, pad_hi, slot, hp)


def _expert_kernel(ea_ref, eb_ref, nu_ref, xs_ref, w1a_ref, w3a_ref, w2a_ref,
                   w1b_ref, w3b_ref, w2b_ref, ys_ref):
    half = ys_ref.shape[1]

    @pl.when(pl.program_id(0) < nu_ref[0])
    def _():
        lo, hi = _unpack_pairs_f32(xs_ref[:, :half])
        x = jnp.concatenate([lo.astype(BF16), hi.astype(BF16)], axis=1)
        gate = lax.bitcast_convert_type(xs_ref[:, half:], F32)

        def ffn(w1_ref, w3_ref, w2_ref):
            a = jnp.dot(x, w1_ref[0], preferred_element_type=F32)
            b = jnp.dot(x, w3_ref[0], preferred_element_type=F32)
            h = (a * jax.nn.sigmoid(a)) * b
            return jnp.dot(h.astype(BF16), w2_ref[0], preferred_element_type=F32)

        y = gate[:, 0:1] * ffn(w1a_ref, w3a_ref, w2a_ref)
        y = y + gate[:, 1:2] * ffn(w1b_ref, w3b_ref, w2b_ref)
        ys_ref[...] = _pack_bf16_pairs(y.astype(BF16))

    @pl.when(pl.program_id(0) >= nu_ref[0])
    def _():
        ys_ref[...] = jnp.zeros_like(ys_ref)


def _experts(xs, block_ea, block_eb, n_used, w1, w3, w2):
    cap, dp = xs.shape
    n_blocks = cap // MOE_BLOCK
    d, de = w1.shape[1:]
    used = lambda i, nu: jnp.minimum(i, nu[0] - 1)
    blk = lambda i, ea, eb, nu: (used(i, nu), 0)
    wa = lambda i, ea, eb, nu: (ea[used(i, nu)], 0, 0)
    wb = lambda i, ea, eb, nu: (eb[used(i, nu)], 0, 0)
    grid_spec = pltpu.PrefetchScalarGridSpec(
        num_scalar_prefetch=3,
        grid=(n_blocks,),
        in_specs=[
            pl.BlockSpec((MOE_BLOCK, dp), blk),
            pl.BlockSpec((1, d, de), wa),
            pl.BlockSpec((1, d, de), wa),
            pl.BlockSpec((1, de, d), wa),
            pl.BlockSpec((1, d, de), wb),
            pl.BlockSpec((1, d, de), wb),
            pl.BlockSpec((1, de, d), wb),
        ],
        out_specs=pl.BlockSpec((MOE_BLOCK, d // 2), lambda i, ea, eb, nu: (i, 0)),
    )
    return pl.pallas_call(
        _expert_kernel,
        grid_spec=grid_spec,
        out_shape=jax.ShapeDtypeStruct((cap, d // 2), U32),
        compiler_params=_cparams(("arbitrary",)),
        name="experts",
    )(block_ea, block_eb, n_used, xs, w1, w3, w2, w1, w3, w2)


C_TM = 512


def _combine_kernel(slot_ref, slotn_ref, x_ref, mod_ref, modn_ref, gn_ref, ys_ref,
                    *rest, final):
    if final:
        o_ref, buf_ref, sem = rest
    else:
        o_ref, hn_ref, buf_ref, sem = rest
    tm = x_ref.shape[0]
    step = pl.program_id(0)
    last = pl.num_programs(0) - 1
    cur = step % 2
    nxt = 1 - cur

    def request(sl_ref, b):
        def body(i, carry):
            for j in range(ISSUE_UNROLL):
                r = i * ISSUE_UNROLL + j
                _row_copy(ys_ref, sl_ref[0, r], buf_ref.at[b], r, sem.at[b]).start(priority=j % 2)
            return carry

        lax.fori_loop(0, tm // ISSUE_UNROLL, body, 0)

    def wait_buffer(b):
        pltpu.make_async_copy(ys_ref.at[pl.ds(0, tm), :], buf_ref.at[b], sem.at[b]).wait()

    @pl.when(step == 0)
    def _():
        request(slot_ref, 0)

    request(slotn_ref, nxt)
    wait_buffer(cur)

    g2 = mod_ref[0, 5:6, :]
    gn = gn_ref[...]
    shift = modn_ref[0, 0:1, :]
    scale = modn_ref[0, 1:2, :]

    def rows_body(i, carry):
        rows = pl.ds(pl.multiple_of(i * NORM_ROWS, NORM_ROWS), NORM_ROWS)
        y = jnp.concatenate(_unpack_pairs_f32(buf_ref[cur, rows, :]), axis=1)
        out = x_ref[rows, :] + g2 * y
        if final:
            ms = jnp.mean(out * out, axis=-1, keepdims=True)
            o_ref[rows, :] = out * lax.rsqrt(ms + EPS) * gn
        else:
            o_ref[rows, :] = out
            hn_ref[rows, :] = _norm_mod_rows(out, gn, shift, scale).astype(hn_ref.dtype)
        return carry

    lax.fori_loop(0, tm // NORM_ROWS, rows_body, 0, unroll=NORM_UNROLL)

    @pl.when(step == last)
    def _():
        wait_buffer(nxt)


def _combine(x1, ys, slot, mod3, modn3, gn, seq, final):
    n, d = x1.shape
    tm = _tile(seq, C_TM)
    per_batch = seq // tm
    row = pl.BlockSpec((tm, d), lambda i, *_: (i, 0))
    mod_spec = pl.BlockSpec((1, N_ADA, d), lambda i, *_: (i // per_batch, 0, 0))
    grid_spec = pltpu.PrefetchScalarGridSpec(
        num_scalar_prefetch=0,
        grid=(n // tm,),
        in_specs=[
            _slot_spec(tm, lambda i: i),
            _slot_spec(tm, lambda i: jnp.minimum(i + 1, n // tm - 1)),
            row,
            mod_spec, mod_spec,
            pl.BlockSpec((1, d), lambda i, *_: (0, 0)),
            pl.BlockSpec(memory_space=pl.ANY),
        ],
        out_specs=row if final else [row, row],
        scratch_shapes=[pltpu.VMEM((2, tm, ys.shape[1]), ys.dtype),
                        pltpu.SemaphoreType.DMA((2,))],
    )
    x_sds = jax.ShapeDtypeStruct((n, d), F32)
    return pl.pallas_call(
        functools.partial(_combine_kernel, final=final),
        grid_spec=grid_spec,
        out_shape=x_sds if final else [x_sds, jax.ShapeDtypeStruct((n, d), BF16)],
        compiler_params=_cparams(("arbitrary",)),
        name="combine",
    )(slot, slot, x1, mod3, modn3, gn, ys)


def _lookup(table, idx):
    table = jnp.asarray(table, I32)
    hit = idx[..., None] == jnp.arange(table.shape[0], dtype=I32)
    return jnp.sum(jnp.where(hit, table, 0), axis=-1).astype(I32)


def _moe(x1, mod3, g2, modn3, gn, wr, router_b, w1, w3, w2, seq, final):
    n, d = x1.shape
    hp, seg_pos, cnt = _route(x1, mod3, g2, wr, router_b, seq)
    counts = cnt[:N_SEG, 0].astype(I32)
    padded = (counts + MOE_BLOCK - 1) // MOE_BLOCK * MOE_BLOCK
    pends = jnp.cumsum(padded).astype(I32)
    pstarts = pends - padded
    n_blocks = -(-n // MOE_BLOCK) + N_SEG
    block_lo = jnp.arange(n_blocks, dtype=I32) * MOE_BLOCK
    block_seg = jnp.minimum(jnp.sum((pends[None, :] <= block_lo[:, None]).astype(I32), axis=1),
                            N_SEG - 1).astype(I32)
    n_used = pends[-1:] // MOE_BLOCK
    slot = (_lookup(pstarts, seg_pos[0]) + seg_pos[1]).reshape(1, n)
    xs = _dispatch(hp, slot, pstarts + counts, pends, n_blocks * MOE_BLOCK)
    ys = _experts(xs, _lookup(SEG_A, block_seg), _lookup(SEG_B, block_seg), n_used, w1, w3, w2)
    return _combine(x1, ys, slot, mod3, modn3, gn, seq, final)


def _fft_split(seq):
    n1_len = 1 << (int(math.log2(seq)) // 2)
    return n1_len, seq // n1_len


def kernel(x, c, positions, ada_w, ada_b, norm1_g, norm2_g, w_in, w_fourier_out, w_attn_out,
           attn_sink, w_out, router_w, router_b, expert_w1, expert_w3, expert_w2, final_norm_g):
    n_batch, seq, d = x.shape
    depth = ada_w.shape[0]
    n1_len, n2_len = _fft_split(seq)
    assert n1_len * n2_len == seq and n1_len % F_SUB == 0 and n2_len % F_SUB == 0
    assert seq % ATT_Q == 0 and seq >= ATT_KW and d == D_MODEL

    mod = _ada_mod(c, ada_w, ada_b).reshape(depth, n_batch, N_ADA, d)
    tabs = _rope_tables(positions)
    consts = _dft_constants(n1_len, n2_len)
    w_in_bf = _reorder_cast_w_in(w_in)
    wr = jnp.pad(router_w, ((0, 0), (0, ROUTER_PAD - N_EXPERTS))).astype(BF16)
    wfo, wao, wout = (w.astype(BF16) for w in (w_fourier_out, w_attn_out, w_out))

    x2 = x.reshape(n_batch * seq, d)
    h = _entry_norm(x2, mod[0], norm1_g[0].reshape(1, d), seq)
    for l in range(depth):
        final = l == depth - 1
        u, (ew1, ew3, ew2) = _in_proj(h, w_in_bf, l, tabs, (expert_w1, expert_w3, expert_w2))
        yf = _fourier(u, n_batch, seq, consts)
        att = _attention(u, attn_sink[l], n_batch, seq)
        x1 = _mix_out(yf, att, u, x2, mod[l], wfo, wao, wout, l, seq)
        gn = (final_norm_g if final else norm1_g[l + 1]).reshape(1, d)
        res = _moe(x1, mod[l], norm2_g[l].reshape(1, d), mod[l if final else l + 1], gn, wr,
                   router_b, ew1, ew3, ew2, seq, final)
        if final:
            x2 = res
        else:
            x2, h = res
    return x2.reshape(n_batch, seq, d)
```

```python
import functools
import math

import numpy as np
import jax
import jax.numpy as jnp
from jax import lax
from jax.experimental import pallas as pl
from jax.experimental.pallas import tpu as pltpu

F32 = jnp.float32
BF16 = jnp.bfloat16
I32 = jnp.int32
U32 = jnp.uint32

D_MODEL = 2048
N_ADA = 6
F_GROUPS = 4
F_GDIM = 256
F_WIDTH = F_GROUPS * F_GDIM
N_HEADS = 16
N_KV = 2
HEAD_DIM = 64
HEADS_PER_KV = N_HEADS // N_KV
Q_WIDTH = N_HEADS * HEAD_DIM
KV_WIDTH = N_KV * HEAD_DIM
ROPE_DIM = HEAD_DIM // 4
ROPE_HALF = ROPE_DIM // 2
ROPE_THETA = 500000.0
WINDOW = 128
N_EXPERTS = 16
N_GROUPS = 4
EPG = N_EXPERTS // N_GROUPS
TOP_K = 2
D_EXPERT = 768
MOE_BLOCK = 512
EPS = 1e-6
NEG_INF = -1e30
IN_WIDTH = F_WIDTH + Q_WIDTH + 2 * KV_WIDTH + 2 * D_MODEL

COL_GF = (F_WIDTH + Q_WIDTH) // D_MODEL
COL_GA = COL_GF + 1
COL_K = (F_WIDTH + Q_WIDTH + 2 * D_MODEL) // KV_WIDTH
COL_V = COL_K + 1

LANES = 128
SUBLANES = 8
BF16_ROWS = 16
VMEM_LIMIT = 56 * 1024 * 1024

ATT_Q = 256
ATT_KW = ATT_Q + 2 * WINDOW
ATT_SUB = 128
ATT_SW = ATT_SUB + 2 * WINDOW
ATT_GROUP = 4


def _cparams(sem, vmem=VMEM_LIMIT):
    return pltpu.CompilerParams(dimension_semantics=sem, vmem_limit_bytes=vmem)


def _tile(n, pref):
    t = min(n, pref)
    while n % t:
        t //= 2
    return t


ADA_TN = 768
ADA_ROWS = 32


def _ada_kernel(cb_ref, w_ref, b_ref, o_ref, *, n_batch):
    tn = w_ref.shape[-1]
    n_chunks = w_ref.shape[1] // ADA_ROWS

    def body(i, accs):
        r = pl.multiple_of(i * ADA_ROWS, ADA_ROWS)
        w = w_ref[0, pl.ds(r, ADA_ROWS), :].reshape(ADA_ROWS // SUBLANES, SUBLANES, tn)
        out = []
        for b in range(n_batch):
            c = cb_ref[b, pl.ds(r, ADA_ROWS), :]
            ca = c * jax.nn.sigmoid(c)
            cr = jnp.tile(ca, (1, tn // LANES)).reshape(ADA_ROWS // SUBLANES, SUBLANES, tn)
            out.append(accs[b] + jnp.sum(w * cr, axis=0))
        return tuple(out)

    accs = lax.fori_loop(0, n_chunks, body,
                         tuple(jnp.zeros((SUBLANES, tn), F32) for _ in range(n_batch)))
    for b in range(n_batch):
        o_ref[0, b:b + 1, :] = jnp.sum(accs[b], axis=0, keepdims=True) + b_ref[0]


def _ada_mod(c, ada_w, ada_b):
    n_layers, d, width = ada_w.shape
    n_batch = c.shape[0]
    cb = jnp.broadcast_to(c[:, :, None], (n_batch, d, LANES))
    return pl.pallas_call(
        functools.partial(_ada_kernel, n_batch=n_batch),
        grid=(n_layers, width // ADA_TN),
        in_specs=[
            pl.BlockSpec((n_batch, d, LANES), lambda l, j: (0, 0, 0)),
            pl.BlockSpec((1, d, ADA_TN), lambda l, j: (l, 0, j)),
            pl.BlockSpec((1, 1, ADA_TN), lambda l, j: (l, 0, j)),
        ],
        out_specs=pl.BlockSpec((1, n_batch, ADA_TN), lambda l, j: (l, 0, j)),
        out_shape=jax.ShapeDtypeStruct((n_layers, n_batch, width), F32),
        compiler_params=_cparams(("arbitrary", "arbitrary")),
        name="ada_mod",
    )(cb, ada_w, ada_b.reshape(n_layers, 1, width))


NORM_ROWS = 16
NORM_UNROLL = 8


def _norm_mod_rows(x, g, shift, scale):
    ms = jnp.mean(x * x, axis=-1, keepdims=True)
    y = x * lax.rsqrt(ms + EPS) * g
    return y * (1.0 + scale) + shift


A_TM = 1024
A_TN = 1280


def _rope_chunks(tn):
    cols = [(c, True) for c in range(F_WIDTH, F_WIDTH + Q_WIDTH, LANES)]
    cols.append((COL_K * KV_WIDTH, False))
    by_tile = {}
    for c, is_q in cols:
        by_tile.setdefault(c // tn, []).append((c % tn, is_q))
    return by_tile


def _a_kernel(h_ref, w_ref, kc_ref, ksa_ref, ksb_ref, qc_ref, qsa_ref, qsb_ref,
              c1_ref, c3_ref, c2_ref, o_ref, b1_ref, b3_ref, b2_ref):
    o_ref[...] = jnp.dot(h_ref[...], w_ref[...], preferred_element_type=F32).astype(o_ref.dtype)
    for tile, chunks in _rope_chunks(o_ref.shape[1]).items():
        @pl.when(pl.program_id(1) == tile)
        def _():
            for col, is_q in chunks:
                c_ref, sa_ref, sb_ref = (qc_ref, qsa_ref, qsb_ref) if is_q else (kc_ref, ksa_ref, ksb_ref)
                t = o_ref[:, col:col + LANES].astype(F32)
                r = (t * c_ref[...] + pltpu.roll(t, LANES - ROPE_HALF, 1) * sa_ref[...]
                     + pltpu.roll(t, ROPE_HALF, 1) * sb_ref[...])
                o_ref[:, col:col + LANES] = r.astype(o_ref.dtype)
    for src, dst in ((c1_ref, b1_ref), (c3_ref, b3_ref), (c2_ref, b2_ref)):
        dst[...] = src[...].astype(dst.dtype)


def _in_proj(h, w_bf, layer, tabs, expert_ws):
    n, d = h.shape
    width = w_bf.shape[-1]
    tm = _tile(n, A_TM)
    tn = _tile(width, A_TN)
    tab_spec = pl.BlockSpec((tm, LANES), lambda i, j: (i, 0))
    nj = width // tn
    steps = (n // tm) * nj
    n_chunks = 1 << (steps.bit_length() - 1)
    chunk_id = lambda i, j: jnp.minimum(i * nj + j, n_chunks - 1)
    side_in, side_out, side_shapes, flats = [], [], [], []
    for w in expert_ws:
        depth, n_exp, rows, cols = w.shape
        per_layer = n_exp * rows
        chunk = per_layer // n_chunks
        assert per_layer % n_chunks == 0 and chunk % BF16_ROWS == 0
        flats.append(w.reshape(depth * per_layer, cols))
        side_in.append(
            pl.BlockSpec((chunk, cols), lambda i, j: (layer * n_chunks + chunk_id(i, j), 0)))
        side_out.append(pl.BlockSpec((chunk, cols), lambda i, j: (chunk_id(i, j), 0)))
        side_shapes.append(jax.ShapeDtypeStruct((per_layer, cols), BF16))
    outs = pl.pallas_call(
        _a_kernel,
        grid=(n // tm, nj),
        in_specs=[
            pl.BlockSpec((tm, d), lambda i, j: (i, 0)),
            pl.BlockSpec((None, d, tn), lambda i, j: (layer, 0, j)),
        ] + [tab_spec] * len(tabs) + side_in,
        out_specs=[pl.BlockSpec((tm, tn), lambda i, j: (i, j))] + side_out,
        out_shape=[jax.ShapeDtypeStruct((n, width), BF16)] + side_shapes,
        compiler_params=_cparams(("arbitrary", "arbitrary")),
        name="in_proj",
    )(h, w_bf, *tabs, *flats)
    casts = [o.reshape(w.shape[1:]) for o, w in zip(outs[1:], expert_ws)]
    return outs[0], casts


W_TILE = 2 * KV_WIDTH


def _cast_kernel(w_ref, o_ref):
    o_ref[...] = w_ref[...].astype(o_ref.dtype)


def _reorder_cast_w_in(w_in):
    depth, d, width = w_in.shape
    kv_tile = (F_WIDTH + Q_WIDTH) // W_TILE
    n_tiles = width // W_TILE
    assert (F_WIDTH + Q_WIDTH) % W_TILE == 0 and width % W_TILE == 0

    def src_tile(j):
        return jnp.where(j < kv_tile, j, jnp.where(j < n_tiles - 1, j + 1, kv_tile))

    return pl.pallas_call(
        _cast_kernel,
        grid=(depth, n_tiles),
        in_specs=[pl.BlockSpec((1, d, W_TILE), lambda l, j: (l, 0, src_tile(j)))],
        out_specs=pl.BlockSpec((1, d, W_TILE), lambda l, j: (l, 0, j)),
        out_shape=jax.ShapeDtypeStruct((depth, d, width), BF16),
        compiler_params=_cparams(("arbitrary", "arbitrary")),
        name="w_in_cast",
    )(w_in)


N_TM = 512


def _norm_rows_to(x_ref, dst_ref, g, shift, scale):
    def body(i, carry):
        r = pl.multiple_of(i * NORM_ROWS, NORM_ROWS)
        h = _norm_mod_rows(x_ref[pl.ds(r, NORM_ROWS), :], g, shift, scale)
        dst_ref[pl.ds(r, NORM_ROWS), :] = h.astype(dst_ref.dtype)
        return carry

    lax.fori_loop(0, x_ref.shape[0] // NORM_ROWS, body, 0, unroll=NORM_UNROLL)


def _norm_kernel(x_ref, mod_ref, g_ref, h_ref):
    _norm_rows_to(x_ref, h_ref, g_ref[...], mod_ref[0, 0:1, :], mod_ref[0, 1:2, :])


def _entry_norm(x2, mod3, g, seq):
    n, d = x2.shape
    tm = _tile(seq, N_TM)
    per_batch = seq // tm
    return pl.pallas_call(
        _norm_kernel,
        grid=(n // tm,),
        in_specs=[
            pl.BlockSpec((tm, d), lambda i: (i, 0)),
            pl.BlockSpec((1, N_ADA, d), lambda i: (i // per_batch, 0, 0)),
            pl.BlockSpec((1, d), lambda i: (0, 0)),
        ],
        out_specs=pl.BlockSpec((tm, d), lambda i: (i, 0)),
        out_shape=jax.ShapeDtypeStruct((n, d), BF16),
        compiler_params=_cparams(("arbitrary",)),
        name="entry_norm",
    )(x2, mod3, g)


LOG2E = math.log2(math.e)
Q_SCALE = HEAD_DIM ** -0.5 * LOG2E


def _rope_tables(positions):
    inv = ROPE_THETA ** (-jnp.arange(0, ROPE_DIM, 2, dtype=F32) / ROPE_DIM)
    ang = positions.astype(F32)[..., None] * inv
    cos = jnp.cos(ang).reshape(-1, ROPE_HALF)
    sin = jnp.sin(ang).reshape(-1, ROPE_HALF)
    n = cos.shape[0]
    rest = HEAD_DIM - ROPE_DIM
    ones = jnp.ones((n, rest), F32)
    z_half = jnp.zeros((n, ROPE_HALF), F32)
    z_rest = jnp.zeros((n, rest), F32)
    reps = LANES // HEAD_DIM
    c = jnp.tile(jnp.concatenate([cos, cos, ones], axis=1), (1, reps))
    sa = jnp.tile(jnp.concatenate([-sin, z_half, z_rest], axis=1), (1, reps))
    sb = jnp.tile(jnp.concatenate([z_half, sin, z_rest], axis=1), (1, reps))
    return c, sa, sb, c * Q_SCALE, sa * Q_SCALE, sb * Q_SCALE


F_SUB = BF16_ROWS


def _dft_constants(n1_len, n2_len):
    cidx = np.arange(F_GDIM)
    ang = 2.0 * np.pi * np.outer(cidx, cidx) / F_GDIM
    wc = np.concatenate([np.cos(ang), -np.sin(ang)], axis=1) / math.sqrt(F_GDIM)
    a = np.arange(n1_len)
    ang1 = 2.0 * np.pi * np.outer(a, a) / n1_len
    fr, fi = np.cos(ang1), -np.sin(ang1)
    fbig = np.block([[fr, -fi], [fi, fr]]) / math.sqrt(n1_len)
    k1 = np.arange(n1_len)[:, None, None]
    k2 = np.arange(n2_len)[None, :, None]
    b = np.arange(n2_len)[None, None, :]
    ang2 = 2.0 * np.pi * (b * k2 / n2_len + b * k1 / (n1_len * n2_len))
    g = np.concatenate([np.cos(ang2), np.sin(ang2)], axis=2) / math.sqrt(n2_len)
    return (jnp.asarray(wc, BF16), jnp.asarray(fbig, BF16), jnp.asarray(g, BF16))


def _f12_kernel(u_ref, wc_ref, fb_ref, t_ref):
    n1_len = u_ref.shape[1]
    x = u_ref[0].reshape(n1_len * F_SUB, F_WIDTH)
    zr, zi = [], []
    for g in range(F_GROUPS):
        r = jnp.dot(x[:, g * F_GDIM:(g + 1) * F_GDIM], wc_ref[...], preferred_element_type=F32)
        zr.append(r[:, :F_GDIM].astype(BF16))
        zi.append(r[:, F_GDIM:].astype(BF16))
    zr = jnp.concatenate(zr, axis=1).reshape(n1_len, F_SUB, F_WIDTH)
    zi = jnp.concatenate(zi, axis=1).reshape(n1_len, F_SUB, F_WIDTH)
    zr = pltpu.einshape("mnc->nmc", zr)
    zi = pltpu.einshape("mnc->nmc", zi)
    for j in range(F_SUB):
        rhs = jnp.concatenate([zr[j], zi[j]], axis=0)
        t = jnp.dot(fb_ref[...], rhs, preferred_element_type=F32)
        t_ref[0, j] = t.astype(BF16).reshape(2, n1_len, F_WIDTH)


def _f3_kernel(t_ref, g_ref, y_ref):
    tr = pltpu.einshape("nkc->knc", t_ref[0, :, 0])
    ti = pltpu.einshape("nkc->knc", t_ref[0, :, 1])
    ys = []
    for j in range(F_SUB):
        rhs = jnp.concatenate([tr[j], ti[j]], axis=0)
        ys.append(jnp.dot(g_ref[j], rhs, preferred_element_type=F32).astype(BF16))
    y_ref[0] = pltpu.einshape("knc->nkc", jnp.stack(ys, axis=0))


def _fourier(u, n_batch, seq, consts):
    wc, fbig, gmat = consts
    n1_len = fbig.shape[0] // 2
    n2_len = seq // n1_len
    u5 = u.reshape(n_batch, n1_len, n2_len, u.shape[-1])
    t = pl.pallas_call(
        _f12_kernel,
        grid=(n_batch, n2_len // F_SUB),
        in_specs=[
            pl.BlockSpec((1, n1_len, F_SUB, F_WIDTH), lambda b, j: (b, 0, j, 0)),
            pl.BlockSpec(wc.shape, lambda b, j: (0, 0)),
            pl.BlockSpec(fbig.shape, lambda b, j: (0, 0)),
        ],
        out_specs=pl.BlockSpec((1, F_SUB, 2, n1_len, F_WIDTH), lambda b, j: (b, j, 0, 0, 0)),
        out_shape=jax.ShapeDtypeStruct((n_batch, n2_len, 2, n1_len, F_WIDTH), BF16),
        compiler_params=_cparams(("arbitrary", "arbitrary")),
        name="fourier_stage1",
    )(u5, wc, fbig)
    y = pl.pallas_call(
        _f3_kernel,
        grid=(n_batch, n1_len // F_SUB),
        in_specs=[
            pl.BlockSpec((1, n2_len, 2, F_SUB, F_WIDTH), lambda b, j: (b, 0, 0, j, 0)),
            pl.BlockSpec((F_SUB, n2_len, 2 * n2_len), lambda b, j: (j, 0, 0)),
        ],
        out_specs=pl.BlockSpec((1, n2_len, F_SUB, F_WIDTH), lambda b, j: (b, 0, j, 0)),
        out_shape=jax.ShapeDtypeStruct((n_batch, n2_len, n1_len, F_WIDTH), BF16),
        compiler_params=_cparams(("arbitrary", "arbitrary")),
        name="fourier_stage2",
    )(t, gmat)
    return y.reshape(n_batch * seq, F_WIDTH)


def _att_kernel(sink_ref, q_ref, k_ref, v_ref, o_ref, kv_ref, *, seq):
    i = pl.program_id(1)
    s0 = i * ATT_Q
    ws = pl.multiple_of(jnp.clip(s0 - WINDOW, 0, seq - ATT_KW), WINDOW)
    off = s0 - ws
    lane = lax.broadcasted_iota(I32, (ATT_KW, LANES), 1)
    low = lane < HEAD_DIM
    for t, ref in enumerate((k_ref, v_ref)):
        w = ref[0, pl.ds(ws, ATT_KW), :].astype(F32)
        w_sw = pltpu.roll(w, HEAD_DIM, 1)
        for h in range(N_KV):
            own, oth = (w, w_sw) if h == 0 else (w_sw, w)
            kv_ref[t, h, 0] = jnp.where(low, own, 0.0).astype(BF16)
            kv_ref[t, h, 1] = jnp.where(low, 0.0, oth).astype(BF16)

    def sub_block(a, carry):
        r0 = pl.multiple_of(a * ATT_SUB, ATT_SUB)
        c0 = pl.multiple_of(
            jnp.clip(((off + r0) // WINDOW) * WINDOW - WINDOW, 0, ATT_KW - ATT_SW), WINDOW)
        row = lax.broadcasted_iota(I32, (ATT_SUB, ATT_SW), 0)
        col = lax.broadcasted_iota(I32, (ATT_SUB, ATT_SW), 1)
        rel = (c0 + col) - (off + r0 + row)
        bias = jnp.where(jnp.abs(rel) <= WINDOW, 0.0, NEG_INF).astype(F32)
        pairs = [(h, m) for h in range(N_KV) for m in range(HEADS_PER_KV // 2)]
        for g0 in range(0, len(pairs), ATT_GROUP):
            group = [(h, m, half) for (h, m) in pairs[g0:g0 + ATT_GROUP] for half in range(2)]
            scores = []
            for h, m, half in group:
                cb = (h * HEADS_PER_KV // 2 + m) * LANES
                q2 = q_ref[pl.ds(r0, ATT_SUB), cb:cb + LANES]
                kk = kv_ref[0, h, half, pl.ds(c0, ATT_SW), :]
                scores.append(lax.dot_general(q2, kk, (((1,), (1,)), ((), ())),
                                              preferred_element_type=F32))
            probs, rdens = [], []
            for (h, m, half), s in zip(group, scores):
                sink = sink_ref[h * HEADS_PER_KV + 2 * m + half] * LOG2E
                s = s + bias
                mx = jnp.maximum(jnp.max(s, axis=-1, keepdims=True), sink)
                p = jnp.exp2(s - mx)
                rdens.append(1.0 / (jnp.sum(p, axis=-1, keepdims=True) + jnp.exp2(sink - mx)))
                probs.append(p.astype(BF16))
            outs = []
            for (h, m, half), p, rden in zip(group, probs, rdens):
                vv = kv_ref[1, h, half, pl.ds(c0, ATT_SW), :]
                outs.append(jnp.dot(p, vv, preferred_element_type=F32) * rden)
            for j, (h, m) in enumerate(pairs[g0:g0 + ATT_GROUP]):
                cb = (h * HEADS_PER_KV // 2 + m) * LANES
                o_ref[pl.ds(r0, ATT_SUB), cb:cb + LANES] = (outs[2 * j] + outs[2 * j + 1]).astype(BF16)
        return carry

    lax.fori_loop(0, ATT_Q // ATT_SUB, sub_block, 0)


def _attention(u, sink, n_batch, seq):
    n = u.shape[0]
    nq = seq // ATT_Q
    u3 = u.reshape(n_batch, seq, u.shape[-1])
    return pl.pallas_call(
        functools.partial(_att_kernel, seq=seq),
        grid=(n_batch, nq),
        in_specs=[
            pl.BlockSpec(memory_space=pltpu.SMEM),
            pl.BlockSpec((ATT_Q, Q_WIDTH), lambda b, i: (b * nq + i, F_WIDTH // Q_WIDTH)),
            pl.BlockSpec((1, seq, KV_WIDTH), lambda b, i: (b, 0, COL_K)),
            pl.BlockSpec((1, seq, KV_WIDTH), lambda b, i: (b, 0, COL_V)),
        ],
        out_specs=pl.BlockSpec((ATT_Q, Q_WIDTH), lambda b, i: (b * nq + i, 0)),
        out_shape=jax.ShapeDtypeStruct((n, Q_WIDTH), BF16),
        scratch_shapes=[pltpu.VMEM((2, N_KV, 2, ATT_KW, LANES), BF16)],
        compiler_params=_cparams(("arbitrary", "arbitrary")),
        name="window_attention",
    )(sink, u, u3, u3)


D_TM = 512
ROUTER_PAD = LANES


def _d_kernel(f_ref, a_ref, gf_ref, ga_ref, x_ref, mod_ref, wfo_ref, wao_ref, wout_ref, x1_ref):
    yf = jnp.dot(f_ref[...], wfo_ref[...], preferred_element_type=F32)
    ya = jnp.dot(a_ref[...], wao_ref[...], preferred_element_type=F32)
    merged = (jax.nn.sigmoid(gf_ref[...].astype(F32)) * yf
              + jax.nn.sigmoid(ga_ref[...].astype(F32)) * ya)
    out = jnp.dot(merged.astype(BF16), wout_ref[...], preferred_element_type=F32)
    x1_ref[...] = x_ref[...] + mod_ref[0, 2:3, :] * out


def _layer_weight_spec(w, layer):
    return pl.BlockSpec((None,) + w.shape[1:], lambda i: (layer, 0, 0),
                        pipeline_mode=pl.Buffered(1))


def _mix_out(yf, att, u, x2, mod3, wfo, wao, wout, layer, seq):
    n, d = x2.shape
    tm = _tile(seq, D_TM)
    per_batch = seq // tm
    row = lambda w: pl.BlockSpec((tm, w), lambda i: (i, 0))
    return pl.pallas_call(
        _d_kernel,
        grid=(n // tm,),
        in_specs=[
            row(F_WIDTH), row(Q_WIDTH),
            pl.BlockSpec((tm, d), lambda i: (i, COL_GF)),
            pl.BlockSpec((tm, d), lambda i: (i, COL_GA)),
            row(d),
            pl.BlockSpec((1, N_ADA, d), lambda i: (i // per_batch, 0, 0)),
            _layer_weight_spec(wfo, layer), _layer_weight_spec(wao, layer),
            _layer_weight_spec(wout, layer),
        ],
        out_specs=row(d),
        out_shape=jax.ShapeDtypeStruct((n, d), F32),
        compiler_params=_cparams(("arbitrary",)),
        name="mix_out",
    )(yf, att, u, u, x2, mod3, wfo, wao, wout)


R_TN = 1024
R_SUB = 512
PAIRS = [(a, b) for a in range(EPG) for b in range(a + 1, EPG)]
N_SEG = N_GROUPS * len(PAIRS)
SEG_PAD = 32
SEG_A = [g * EPG + a for g in range(N_GROUPS) for a, _ in PAIRS]
SEG_B = [g * EPG + b for g in range(N_GROUPS) for _, b in PAIRS]
HI_MASK = 0xFFFF0000


def _pack_bf16_pairs(hb):
    bits = lax.bitcast_convert_type(hb.astype(F32), U32)
    half = hb.shape[1] // 2
    return (bits[:, half:] & U32(HI_MASK)) | (bits[:, :half] >> U32(16))


def _unpack_pairs_f32(xu):
    return (lax.bitcast_convert_type(xu << U32(16), F32),
            lax.bitcast_convert_type(xu & U32(HI_MASK), F32))


def _route_kernel(bias_ref, x_ref, mod_ref, g2_ref, wr_ref, tri_ref,
                  hp_ref, sp_ref, cnt_ref, carry_ref, hb_ref):
    @pl.when(pl.program_id(0) == 0)
    def _():
        carry_ref[...] = jnp.zeros_like(carry_ref)

    tn, d = x_ref.shape
    half = d // 2
    g2 = g2_ref[...]
    shift = mod_ref[0, 3:4, :]
    scale = mod_ref[0, 4:5, :]

    def norm_body(i, c):
        r = pl.multiple_of(i * NORM_ROWS, NORM_ROWS)
        hb = _norm_mod_rows(x_ref[pl.ds(r, NORM_ROWS), :], g2, shift, scale).astype(BF16)
        hb_ref[pl.ds(r, NORM_ROWS), :] = hb
        hp_ref[pl.ds(r, NORM_ROWS), :half] = _pack_bf16_pairs(hb)
        return c

    lax.fori_loop(0, tn // NORM_ROWS, norm_body, 0, unroll=NORM_UNROLL)
    logits = jnp.dot(hb_ref[...], wr_ref[...], preferred_element_type=F32)
    lt = logits.T
    sc = [jax.nn.sigmoid(lt[e:e + 1, :]) for e in range(N_EXPERTS)]
    bz = [sc[e] + bias_ref[e] for e in range(N_EXPERTS)]
    one = jnp.ones((1, tn), F32)
    zero = jnp.zeros((1, tn), F32)
    in_top, gscore = [], []
    for g in range(N_GROUPS):
        v = bz[g * EPG:(g + 1) * EPG]
        rank = [zero] * EPG
        for a in range(EPG):
            for b in range(a + 1, EPG):
                a_wins = jnp.where(v[a] >= v[b], one, zero)
                rank[b] = rank[b] + a_wins
                rank[a] = rank[a] + (one - a_wins)
        top = [jnp.where(rank[a] < TOP_K - 0.5, one, zero) for a in range(EPG)]
        in_top += top
        gs = zero
        for a in range(EPG):
            gs = gs + top[a] * v[a]
        gscore.append(gs)
    sel = []
    for g in range(N_GROUPS):
        ok = one
        for other in range(N_GROUPS):
            if other < g:
                ok = ok * jnp.where(gscore[g] > gscore[other], one, zero)
            elif other > g:
                ok = ok * jnp.where(gscore[g] >= gscore[other], one, zero)
        sel += [in_top[g * EPG + a] * ok for a in range(EPG)]

    segsel = [sel[g * EPG + a] * sel[g * EPG + b] for g in range(N_GROUPS) for a, b in PAIRS]
    selm = jnp.concatenate(segsel + [zero] * (SEG_PAD - N_SEG), axis=0)
    carry = carry_ref[:, 0:1]
    cums = []
    for c in range(tn // R_SUB):
        blk = selm[:, c * R_SUB:(c + 1) * R_SUB]
        cums.append(jnp.dot(blk.astype(BF16), tri_ref[...], preferred_element_type=F32) + carry)
        carry = carry + jnp.sum(blk, axis=1, keepdims=True)
    cum = jnp.concatenate(cums, axis=1)
    carry_ref[...] = jnp.broadcast_to(carry, carry_ref.shape)
    cnt_ref[...] = jnp.broadcast_to(carry, cnt_ref.shape)

    seg = zero
    pos = zero
    for s_id in range(N_SEG):
        seg = seg + segsel[s_id] * float(s_id)
        pos = pos + segsel[s_id] * cum[s_id:s_id + 1, :]
    sp_ref[0:1, :] = seg.astype(I32)
    sp_ref[1:2, :] = pos.astype(I32)

    before = zero
    score = [zero, zero]
    for e in range(N_EXPERTS):
        score[0] = score[0] + sel[e] * jnp.where(before < 0.5, one, zero) * sc[e]
        score[1] = score[1] + sel[e] * jnp.where(before >= 0.5, one, zero) * sc[e]
        before = before + sel[e]
    tot = score[0] + score[1]
    rows = lax.broadcasted_iota(I32, (LANES, tn), 0)
    wmat = jnp.where(rows == 0, score[0] / tot, jnp.where(rows == 1, score[1] / tot, 0.0))
    hp_ref[:, half:] = lax.bitcast_convert_type(wmat.T, U32)


def _route(x1, mod3, g2, wr, router_b, seq):
    n, d = x1.shape
    tn = _tile(seq, R_TN)
    per_batch = seq // tn
    tri = jnp.asarray(np.triu(np.ones((R_SUB, R_SUB), np.float32), k=1), BF16)
    return pl.pallas_call(
        _route_kernel,
        grid=(n // tn,),
        in_specs=[
            pl.BlockSpec(memory_space=pltpu.SMEM),
            pl.BlockSpec((tn, d), lambda i: (i, 0)),
            pl.BlockSpec((1, N_ADA, d), lambda i: (i // per_batch, 0, 0)),
            pl.BlockSpec((1, d), lambda i: (0, 0)),
            pl.BlockSpec(wr.shape, lambda i: (0, 0)),
            pl.BlockSpec((R_SUB, R_SUB), lambda i: (0, 0)),
        ],
        out_specs=[pl.BlockSpec((tn, d // 2 + LANES), lambda i: (i, 0)),
                   pl.BlockSpec((2, tn), lambda i: (0, i)),
                   pl.BlockSpec((SEG_PAD, LANES), lambda i: (0, 0))],
        out_shape=[jax.ShapeDtypeStruct((n, d // 2 + LANES), U32),
                   jax.ShapeDtypeStruct((2, n), I32),
                   jax.ShapeDtypeStruct((SEG_PAD, LANES), F32)],
        scratch_shapes=[pltpu.VMEM((SEG_PAD, LANES), F32), pltpu.VMEM((tn, d), BF16)],
        compiler_params=_cparams(("arbitrary",)),
        name="route",
    )(router_b, x1, mod3, g2, wr, tri)


P_TM = 2048
P_ZROWS = 256
ISSUE_UNROLL = 8


def _row_copy(src, src_row, dst, dst_row, sem):
    return pltpu.make_async_copy(src.at[pl.ds(src_row, 1), :], dst.at[pl.ds(dst_row, 1), :], sem)


def _dispatch_kernel(padlo_ref, padhi_ref, slot_ref, h_ref, xs_ref, zrow_ref, sem, zsem):
    tm = h_ref.shape[0]

    def issue(i, carry):
        for j in range(ISSUE_UNROLL):
            r = i * ISSUE_UNROLL + j
            _row_copy(h_ref, r, xs_ref, slot_ref[0, r], sem).start(priority=j % 2)
        return carry

    lax.fori_loop(0, tm // ISSUE_UNROLL, issue, 0)

    @pl.when(pl.program_id(0) == pl.num_programs(0) - 1)
    def _():
        zrow_ref[...] = jnp.zeros_like(zrow_ref)
        zrows = zrow_ref.shape[0]

        def zissue(r, carry):
            _row_copy(zrow_ref, 0, xs_ref, r, zsem).start()
            return carry

        def zwait(r, carry):
            _row_copy(zrow_ref, 0, xs_ref, r, zsem).wait()
            return carry

        def chunked(first, count, rows):
            def chunk(j):
                start = pl.multiple_of(first + j * rows, rows)
                return pltpu.make_async_copy(zrow_ref.at[pl.ds(0, rows), :],
                                             xs_ref.at[pl.ds(start, rows), :], zsem)

            lax.fori_loop(0, count, lambda j, c: (chunk(j).start(), c)[1], 0)
            lax.fori_loop(0, count, lambda j, c: (chunk(j).wait(), c)[1], 0)

        for s_id in range(N_SEG):
            lo = padlo_ref[s_id]
            hi = padhi_ref[s_id]
            aligned = jnp.minimum(hi, (lo + SUBLANES - 1) // SUBLANES * SUBLANES)
            lax.fori_loop(lo, aligned, zissue, 0)
            lax.fori_loop(lo, aligned, zwait, 0)
            chunked(aligned, (hi - aligned) // SUBLANES, SUBLANES)

        chunked(padhi_ref[N_SEG - 1], (xs_ref.shape[0] - padhi_ref[N_SEG - 1]) // zrows, zrows)

    pltpu.make_async_copy(h_ref, xs_ref.at[pl.ds(0, tm), :], sem).wait()


def _slot_spec(tm, index):
    return pl.BlockSpec((1, tm), lambda i, *_: (0, index(i)), memory_space=pltpu.SMEM)


def _dispatch(hp, slot, pad_lo, pad_hi, cap):
    n, dp = hp.shape
    tm = _tile(n, P_TM)
    grid_spec = pltpu.PrefetchScalarGridSpec(
        num_scalar_prefetch=2,
        grid=(n // tm,),
        in_specs=[_slot_spec(tm, lambda i: i), pl.BlockSpec((tm, dp), lambda i, *_: (i, 0))],
        out_specs=pl.BlockSpec(memory_space=pl.ANY),
        scratch_shapes=[pltpu.VMEM((P_ZROWS, dp), hp.dtype), pltpu.SemaphoreType.DMA(()),
                        pltpu.SemaphoreType.DMA(())],
    )
    return pl.pallas_call(
        _dispatch_kernel,
        grid_spec=grid_spec,
        out_shape=jax.ShapeDtypeStruct((cap, dp), hp.dtype),
        compiler_params=_cparams(("arbitrary",)),
        name="dispatch",
    )(pad_lo, pad_hi, slot, hp)


def _expert_kernel(ea_ref, eb_ref, nu_ref, valid_ref, xs_ref, w1a_ref, w3a_ref, w2a_ref,
                   w1b_ref, w3b_ref, w2b_ref, ys_ref):
    half = ys_ref.shape[1]
    step = pl.program_id(0)
    used = step < nu_ref[0]
    valid = valid_ref[step]

    def run(rows):
        lo, hi = _unpack_pairs_f32(xs_ref[:rows, :half])
        x = jnp.concatenate([lo.astype(BF16), hi.astype(BF16)], axis=1)
        gate = lax.bitcast_convert_type(xs_ref[:rows, half:], F32)

        def ffn(w1_ref, w3_ref, w2_ref):
            a = jnp.dot(x, w1_ref[0], preferred_element_type=F32)
            b = jnp.dot(x, w3_ref[0], preferred_element_type=F32)
            h = (a * jax.nn.sigmoid(a)) * b
            return jnp.dot(h.astype(BF16), w2_ref[0], preferred_element_type=F32)

        y = gate[:, 0:1] * ffn(w1a_ref, w3a_ref, w2a_ref)
        y = y + gate[:, 1:2] * ffn(w1b_ref, w3b_ref, w2b_ref)
        ys_ref[:rows, :] = _pack_bf16_pairs(y.astype(BF16))

    @pl.when(used & (valid > MOE_BLOCK // 2))
    def _():
        run(MOE_BLOCK)

    @pl.when(used & (valid <= MOE_BLOCK // 2))
    def _():
        run(MOE_BLOCK // 2)
        ys_ref[MOE_BLOCK // 2:, :] = jnp.zeros((MOE_BLOCK // 2, half), ys_ref.dtype)

    @pl.when(jnp.logical_not(used))
    def _():
        ys_ref[...] = jnp.zeros_like(ys_ref)


def _experts(xs, block_ea, block_eb, n_used, valid, w1, w3, w2):
    cap, dp = xs.shape
    n_blocks = cap // MOE_BLOCK
    d, de = w1.shape[1:]
    used = lambda i, nu: jnp.minimum(i, nu[0] - 1)
    blk = lambda i, ea, eb, nu, vl: (used(i, nu), 0)
    wa = lambda i, ea, eb, nu, vl: (ea[used(i, nu)], 0, 0)
    wb = lambda i, ea, eb, nu, vl: (eb[used(i, nu)], 0, 0)
    grid_spec = pltpu.PrefetchScalarGridSpec(
        num_scalar_prefetch=4,
        grid=(n_blocks,),
        in_specs=[
            pl.BlockSpec((MOE_BLOCK, dp), blk),
            pl.BlockSpec((1, d, de), wa),
            pl.BlockSpec((1, d, de), wa),
            pl.BlockSpec((1, de, d), wa),
            pl.BlockSpec((1, d, de), wb),
            pl.BlockSpec((1, d, de), wb),
            pl.BlockSpec((1, de, d), wb),
        ],
        out_specs=pl.BlockSpec((MOE_BLOCK, d // 2), lambda i, ea, eb, nu, vl: (i, 0)),
    )
    return pl.pallas_call(
        _expert_kernel,
        grid_spec=grid_spec,
        out_shape=jax.ShapeDtypeStruct((cap, d // 2), U32),
        compiler_params=_cparams(("arbitrary",)),
        name="experts",
    )(block_ea, block_eb, n_used, valid, xs, w1, w3, w2, w1, w3, w2)


C_TM = 512


def _combine_kernel(slot_ref, slotn_ref, x_ref, mod_ref, modn_ref, gn_ref, ys_ref,
                    *rest, final):
    if final:
        o_ref, buf_ref, sem = rest
    else:
        o_ref, hn_ref, buf_ref, sem = rest
    tm = x_ref.shape[0]
    step = pl.program_id(0)
    last = pl.num_programs(0) - 1
    cur = step % 2
    nxt = 1 - cur

    def request(sl_ref, b):
        def body(i, carry):
            for j in range(ISSUE_UNROLL):
                r = i * ISSUE_UNROLL + j
                _row_copy(ys_ref, sl_ref[0, r], buf_ref.at[b], r, sem.at[b]).start(priority=j % 2)
            return carry

        lax.fori_loop(0, tm // ISSUE_UNROLL, body, 0)

    def wait_buffer(b):
        pltpu.make_async_copy(ys_ref.at[pl.ds(0, tm), :], buf_ref.at[b], sem.at[b]).wait()

    @pl.when(step == 0)
    def _():
        request(slot_ref, 0)

    request(slotn_ref, nxt)
    wait_buffer(cur)

    g2 = mod_ref[0, 5:6, :]
    gn = gn_ref[...]
    shift = modn_ref[0, 0:1, :]
    scale = modn_ref[0, 1:2, :]

    def rows_body(i, carry):
        rows = pl.ds(pl.multiple_of(i * NORM_ROWS, NORM_ROWS), NORM_ROWS)
        y = jnp.concatenate(_unpack_pairs_f32(buf_ref[cur, rows, :]), axis=1)
        out = x_ref[rows, :] + g2 * y
        if final:
            ms = jnp.mean(out * out, axis=-1, keepdims=True)
            o_ref[rows, :] = out * lax.rsqrt(ms + EPS) * gn
        else:
            o_ref[rows, :] = out
            hn_ref[rows, :] = _norm_mod_rows(out, gn, shift, scale).astype(hn_ref.dtype)
        return carry

    lax.fori_loop(0, tm // NORM_ROWS, rows_body, 0, unroll=NORM_UNROLL)

    @pl.when(step == last)
    def _():
        wait_buffer(nxt)


def _combine(x1, ys, slot, mod3, modn3, gn, seq, final):
    n, d = x1.shape
    tm = _tile(seq, C_TM)
    per_batch = seq // tm
    row = pl.BlockSpec((tm, d), lambda i, *_: (i, 0))
    mod_spec = pl.BlockSpec((1, N_ADA, d), lambda i, *_: (i // per_batch, 0, 0))
    grid_spec = pltpu.PrefetchScalarGridSpec(
        num_scalar_prefetch=0,
        grid=(n // tm,),
        in_specs=[
            _slot_spec(tm, lambda i: i),
            _slot_spec(tm, lambda i: jnp.minimum(i + 1, n // tm - 1)),
            row,
            mod_spec, mod_spec,
            pl.BlockSpec((1, d), lambda i, *_: (0, 0)),
            pl.BlockSpec(memory_space=pl.ANY),
        ],
        out_specs=row if final else [row, row],
        scratch_shapes=[pltpu.VMEM((2, tm, ys.shape[1]), ys.dtype),
                        pltpu.SemaphoreType.DMA((2,))],
    )
    x_sds = jax.ShapeDtypeStruct((n, d), F32)
    return pl.pallas_call(
        functools.partial(_combine_kernel, final=final),
        grid_spec=grid_spec,
        out_shape=x_sds if final else [x_sds, jax.ShapeDtypeStruct((n, d), BF16)],
        compiler_params=_cparams(("arbitrary",)),
        name="combine",
    )(slot, slot, x1, mod3, modn3, gn, ys)


def _lookup(table, idx):
    table = jnp.asarray(table, I32)
    hit = idx[..., None] == jnp.arange(table.shape[0], dtype=I32)
    return jnp.sum(jnp.where(hit, table, 0), axis=-1).astype(I32)


def _moe(x1, mod3, g2, modn3, gn, wr, router_b, w1, w3, w2, seq, final):
    n, d = x1.shape
    hp, seg_pos, cnt = _route(x1, mod3, g2, wr, router_b, seq)
    counts = cnt[:N_SEG, 0].astype(I32)
    padded = (counts + MOE_BLOCK - 1) // MOE_BLOCK * MOE_BLOCK
    pends = jnp.cumsum(padded).astype(I32)
    pstarts = pends - padded
    n_blocks = -(-n // MOE_BLOCK) + N_SEG
    block_lo = jnp.arange(n_blocks, dtype=I32) * MOE_BLOCK
    block_seg = jnp.minimum(jnp.sum((pends[None, :] <= block_lo[:, None]).astype(I32), axis=1),
                            N_SEG - 1).astype(I32)
    n_used = pends[-1:] // MOE_BLOCK
    slot = (_lookup(pstarts, seg_pos[0]) + seg_pos[1]).reshape(1, n)
    xs = _dispatch(hp, slot, pstarts + counts, pends, n_blocks * MOE_BLOCK)
    valid = jnp.clip(_lookup(pstarts + counts, block_seg) - block_lo, 0, MOE_BLOCK).astype(I32)
    ys = _experts(xs, _lookup(SEG_A, block_seg), _lookup(SEG_B, block_seg), n_used, valid,
                  w1, w3, w2)
    return _combine(x1, ys, slot, mod3, modn3, gn, seq, final)


def _fft_split(seq):
    n1_len = 1 << (int(math.log2(seq)) // 2)
    return n1_len, seq // n1_len


def kernel(x, c, positions, ada_w, ada_b, norm1_g, norm2_g, w_in, w_fourier_out, w_attn_out,
           attn_sink, w_out, router_w, router_b, expert_w1, expert_w3, expert_w2, final_norm_g):
    n_batch, seq, d = x.shape
    depth = ada_w.shape[0]
    n1_len, n2_len = _fft_split(seq)
    assert n1_len * n2_len == seq and n1_len % F_SUB == 0 and n2_len % F_SUB == 0
    assert seq % ATT_Q == 0 and seq >= ATT_KW and d == D_MODEL

    mod = _ada_mod(c, ada_w, ada_b).reshape(depth, n_batch, N_ADA, d)
    tabs = _rope_tables(positions)
    consts = _dft_constants(n1_len, n2_len)
    w_in_bf = _reorder_cast_w_in(w_in)
    wr = jnp.pad(router_w, ((0, 0), (0, ROUTER_PAD - N_EXPERTS))).astype(BF16)
    wfo, wao, wout = (w.astype(BF16) for w in (w_fourier_out, w_attn_out, w_out))

    x2 = x.reshape(n_batch * seq, d)
    h = _entry_norm(x2, mod[0], norm1_g[0].reshape(1, d), seq)
    for l in range(depth):
        final = l == depth - 1
        u, (ew1, ew3, ew2) = _in_proj(h, w_in_bf, l, tabs, (expert_w1, expert_w3, expert_w2))
        yf = _fourier(u, n_batch, seq, consts)
        att = _attention(u, attn_sink[l], n_batch, seq)
        x1 = _mix_out(yf, att, u, x2, mod[l], wfo, wao, wout, l, seq)
        gn = (final_norm_g if final else norm1_g[l + 1]).reshape(1, d)
        res = _moe(x1, mod[l], norm2_g[l].reshape(1, d), mod[l if final else l + 1], gn, wr,
                   router_b, ew1, ew3, ew2, seq, final)
        if final:
            x2 = res
        else:
            x2, h = res
    return x2.reshape(n_batch, seq, d)
```
